```python
import jax, jax.numpy as jnp
from jax import lax
import numpy as np

D_MODEL = 2048
BATCH = 4
SEQ = 2048
DEPTH = 2

CTX_LEN = 256
GRID_W = 64
POOL_WINDOWS = (2, 4, 8, 16)
POOL_GROUP = D_MODEL // 8
POOL_WIDTH = POOL_GROUP * len(POOL_WINDOWS)
CONV_WIDTH = D_MODEL // 2
N_HEADS = 16
QK_NOPE = 128
QK_ROPE = 64
ROPE_AXIS = QK_ROPE // 2
V_DIM = 128
QK_DIM = QK_NOPE + QK_ROPE
Q_LORA = 512
KV_LORA = 512
ROPE_THETA = 10000.0
ATTN_SCALE = QK_DIM ** -0.5
Q_BLOCK = 128
D_FF = 5632
N_BRANCH = 3
EPS = 1e-6
OFF_A = 0
OFF_B = OFF_A + POOL_WIDTH
OFF_Q = OFF_B + 3 * CONV_WIDTH
OFF_KV = OFF_Q + Q_LORA
OFF_G = OFF_KV + KV_LORA + QK_ROPE
IN_COLS = OFF_G + N_BRANCH * D_MODEL

kernel_name = 'hybrid_pool_conv_mla_prefix_dit_block'


def rmsnorm(x, g):
    xf = x.astype(jnp.float32)
    y = xf * lax.rsqrt(jnp.mean(xf * xf, axis=-1, keepdims=True) + EPS)
    return (y * g.astype(jnp.float32)).astype(x.dtype)


def modulate(x, g, shift, scale):
    return rmsnorm(x, g) * (1 + scale) + shift


def conv3_centred(z, w):
    zp = jnp.pad(z, ((0, 0), (1, 1), (0, 0)))
    return zp[:, :-2] * w[0] + zp[:, 1:-1] * w[1] + zp[:, 2:] * w[2]


def multiscale_pool(a, pool_w, pool_scale):
    Bn, L, _ = a.shape
    af = a.astype(jnp.float32)
    prefix = jnp.concatenate([jnp.zeros((Bn, 1, POOL_WIDTH), jnp.float32), jnp.cumsum(af, axis=1)], axis=1)
    t = jnp.arange(L)
    outs = []
    for gi, w in enumerate(POOL_WINDOWS):
        lo = jnp.clip(t - w // 2, 0, L)
        hi = jnp.clip(t + w // 2, 0, L)
        sl = slice(gi * POOL_GROUP, (gi + 1) * POOL_GROUP)
        pg = prefix[:, :, sl]
        cnt = (hi - lo).astype(jnp.float32)[None, :, None]
        outs.append((pg[:, hi] - pg[:, lo]) / cnt - af[:, :, sl])
    pooled = jnp.stack(outs, axis=2).astype(a.dtype)
    mixed = jnp.einsum('blgc,gcd->blgd', pooled, pool_w)
    return mixed.reshape(Bn, L, POOL_WIDTH) * pool_scale


def axial_rope_tables(length):
    rows = length // GRID_W
    row = jnp.repeat(jnp.arange(rows, dtype=jnp.int32), GRID_W).astype(jnp.float32)
    col = jnp.tile(jnp.arange(GRID_W, dtype=jnp.int32), rows).astype(jnp.float32)
    inv = ROPE_THETA ** (-jnp.arange(0, ROPE_AXIS, 2, dtype=jnp.float32) / ROPE_AXIS)
    ang_r = row[:, None] * inv[None]
    ang_c = col[:, None] * inv[None]
    return (jnp.cos(ang_r), jnp.sin(ang_r), jnp.cos(ang_c), jnp.sin(ang_c))


def rotate_half(x, cos, sin):
    x1, x2 = jnp.split(x, 2, axis=-1)
    return jnp.concatenate([x1 * cos - x2 * sin, x1 * sin + x2 * cos], axis=-1)


def apply_axial_rope(x, tables):
    cr, sr, cc, sc = tables
    xf = x.astype(jnp.float32)
    out = jnp.concatenate([rotate_half(xf[..., :ROPE_AXIS], cr, sr),
                           rotate_half(xf[..., ROPE_AXIS:], cc, sc)], axis=-1)
    return out.astype(x.dtype)


def mla_q(zq, q_lora_g, w_uq, q_head_g, rope):
    cq = rmsnorm(zq, q_lora_g)
    q = rmsnorm(jnp.einsum('blr,rhd->bhld', cq, w_uq), q_head_g)
    if rope is not None:
        q = jnp.concatenate([q[..., :QK_NOPE], apply_axial_rope(q[..., QK_NOPE:], rope)], axis=-1)
    return q


def mla_kv(zkv, kv_lora_g, w_ukv, k_head_g, rope):
    ckv = rmsnorm(zkv[..., :KV_LORA], kv_lora_g)
    k_rope = zkv[..., KV_LORA:]
    kv = jnp.einsum('blr,rhd->bhld', ckv, w_ukv)
    k_nope, v = kv[..., :QK_NOPE], kv[..., QK_NOPE:]
    Bn, H, L, _ = k_nope.shape
    k = jnp.concatenate([k_nope, jnp.broadcast_to(k_rope[:, None], (Bn, H, L, QK_ROPE))], axis=-1)
    k = rmsnorm(k, k_head_g)
    if rope is not None:
        k = jnp.concatenate([k[..., :QK_NOPE], apply_axial_rope(k[..., QK_NOPE:], rope)], axis=-1)
    return k, v


def attend(q, k, v):
    s = jnp.einsum('bhqd,bhkd->bhqk', q, k).astype(jnp.float32) * ATTN_SCALE
    p = jax.nn.softmax(s, axis=-1).astype(v.dtype)
    return jnp.einsum('bhqk,bhkd->bhqd', p, v)


def blocked_attend(q, k, v):
    Bn, H, L, Dq = q.shape
    nb = L // Q_BLOCK
    qb = q.reshape(Bn, H, nb, Q_BLOCK, Dq).transpose(2, 0, 1, 3, 4)
    ob = lax.map(lambda qi: attend(qi, k, v), qb)
    return ob.transpose(1, 2, 0, 3, 4).reshape(Bn, H, L, V_DIM)


def token_mixers(h, lp, rope, ctx_kv):
    Bn, L, _ = h.shape
    z = h @ lp['w_in']
    y_a = multiscale_pool(z[..., OFF_A:OFF_B], lp['pool_w'], lp['pool_scale']) @ lp['w_branch_a']
    gate_b, gate_c, xin = jnp.split(z[..., OFF_B:OFF_Q], 3, axis=-1)
    y_b = (gate_b * conv3_centred(gate_c * xin, lp['conv_w'])) @ lp['w_branch_b']
    q = mla_q(z[..., OFF_Q:OFF_KV], lp['q_lora_g'], lp['w_uq'], lp['q_head_g'], rope)
    k, v = mla_kv(z[..., OFF_KV:OFF_G], lp['kv_lora_g'], lp['w_ukv'], lp['k_head_g'], rope)
    if ctx_kv is None:
        o = attend(q, k, v)
    else:
        k_ctx, v_ctx = ctx_kv
        o = blocked_attend(q, jnp.concatenate([k, k_ctx], axis=2), jnp.concatenate([v, v_ctx], axis=2))
    y_c = o.transpose(0, 2, 1, 3).reshape(Bn, L, N_HEADS * V_DIM) @ lp['w_branch_c']
    g = jax.nn.sigmoid(z[..., OFF_G:].astype(jnp.float32)).astype(h.dtype).reshape(Bn, L, N_BRANCH, D_MODEL)
    merged = g[..., 0, :] * y_a + g[..., 1, :] * y_b + g[..., 2, :] * y_c
    return merged @ lp['w_out'], (k, v)


def conv_ffn(h, w_up, conv, w_down):
    u, v = jnp.split(h @ w_up, 2, axis=-1)
    return (jax.nn.silu(conv3_centred(u, conv)) * v) @ w_down


def setup_inputs(seed: int = 0) -> dict:
    key = jax.random.key(seed)
    ks = jax.random.split(key, 32)
    f32 = jnp.float32

    def nrm(k, shape, scale):
        return jax.random.normal(k, shape, f32) * scale

    def gain(k, shape):
        return 1.0 + 0.05 * jax.random.normal(k, shape, f32)

    Dp = DEPTH
    return {
        'x': nrm(ks[0], (BATCH, SEQ, D_MODEL), 1.0),
        'c': nrm(ks[1], (BATCH, D_MODEL), 1.0),
        'ctx': nrm(ks[2], (BATCH, CTX_LEN, D_MODEL), 1.0),
        'c_ctx': nrm(ks[3], (D_MODEL,), 1.0),
        'norm1_g': gain(ks[4], (Dp, D_MODEL)),
        'norm2_g': gain(ks[5], (Dp, D_MODEL)),
        'w_mod': nrm(ks[6], (Dp, D_MODEL, 6 * D_MODEL), 0.5 * D_MODEL ** -0.5),
        'b_mod': nrm(ks[7], (Dp, 6 * D_MODEL), 0.01),
        'w_in': nrm(ks[8], (Dp, D_MODEL, IN_COLS), D_MODEL ** -0.5),
        'pool_w': nrm(ks[9], (Dp, len(POOL_WINDOWS), POOL_GROUP, POOL_GROUP), POOL_GROUP ** -0.5),
        'pool_scale': gain(ks[10], (Dp, POOL_WIDTH)),
        'conv_w': nrm(ks[11], (Dp, 3, CONV_WIDTH), 3 ** -0.5),
        'q_lora_g': gain(ks[12], (Dp, Q_LORA)),
        'w_uq': nrm(ks[13], (Dp, Q_LORA, N_HEADS, QK_DIM), Q_LORA ** -0.5),
        'kv_lora_g': gain(ks[14], (Dp, KV_LORA)),
        'w_ukv': nrm(ks[15], (Dp, KV_LORA, N_HEADS, QK_NOPE + V_DIM), KV_LORA ** -0.5),
        'q_head_g': gain(ks[16], (Dp, QK_DIM)),
        'k_head_g': gain(ks[17], (Dp, QK_DIM)),
        'w_branch_a': nrm(ks[18], (Dp, POOL_WIDTH, D_MODEL), POOL_WIDTH ** -0.5),
        'w_branch_b': nrm(ks[19], (Dp, CONV_WIDTH, D_MODEL), CONV_WIDTH ** -0.5),
        'w_branch_c': nrm(ks[20], (Dp, N_HEADS * V_DIM, D_MODEL), (N_HEADS * V_DIM) ** -0.5),
        'w_out': nrm(ks[21], (Dp, D_MODEL, D_MODEL), D_MODEL ** -0.5),
        'w_ffn_up': nrm(ks[22], (Dp, D_MODEL, 2 * D_FF), D_MODEL ** -0.5),
        'ffn_conv': nrm(ks[23], (Dp, 3, D_FF), 3 ** -0.5),
        'w_ffn_down': nrm(ks[24], (Dp, D_FF, D_MODEL), D_FF ** -0.5),
    }


def reference(x, c, ctx, c_ctx, norm1_g, norm2_g, w_mod, b_mod, w_in, pool_w, pool_scale, conv_w,
              q_lora_g, w_uq, kv_lora_g, w_ukv, q_head_g, k_head_g, w_branch_a, w_branch_b,
              w_branch_c, w_out, w_ffn_up, ffn_conv, w_ffn_down):
    rope = axial_rope_tables(x.shape[1])
    for i in range(DEPTH):
        last = i == DEPTH - 1
        lp = dict(w_in=w_in[i], pool_w=pool_w[i], pool_scale=pool_scale[i], conv_w=conv_w[i],
                  q_lora_g=q_lora_g[i], w_uq=w_uq[i], kv_lora_g=kv_lora_g[i], w_ukv=w_ukv[i],
                  q_head_g=q_head_g[i], k_head_g=k_head_g[i], w_branch_a=w_branch_a[i],
                  w_branch_b=w_branch_b[i], w_branch_c=w_branch_c[i], w_out=w_out[i])
        mod_x = jnp.split((jax.nn.silu(c) @ w_mod[i] + b_mod[i])[:, None, :], 6, axis=-1)
        mod_c = jnp.split(jax.nn.silu(c_ctx) @ w_mod[i] + b_mod[i], 6, axis=-1)
        sh1x, sc1x, g1x, sh2x, sc2x, g2x = mod_x
        sh1c, sc1c, g1c, sh2c, sc2c, g2c = mod_c
        hc = modulate(ctx, norm1_g[i], sh1c, sc1c)
        if last:
            k_c, v_c = mla_kv(hc @ lp['w_in'][:, OFF_KV:OFF_G], lp['kv_lora_g'], lp['w_ukv'],
                              lp['k_head_g'], None)
        else:
            ctx_mix, (k_c, v_c) = token_mixers(hc, lp, None, None)
            ctx = ctx + g1c * ctx_mix
            ctx = ctx + g2c * conv_ffn(modulate(ctx, norm2_g[i], sh2c, sc2c), w_ffn_up[i], ffn_conv[i], w_ffn_down[i])
        hx = modulate(x, norm1_g[i], sh1x, sc1x)
        x_mix, _ = token_mixers(hx, lp, rope, (k_c, v_c))
        x = x + g1x * x_mix
        x = x + g2x * conv_ffn(modulate(x, norm2_g[i], sh2x, sc2x), w_ffn_up[i], ffn_conv[i], w_ffn_down[i])
    return x
```

```python
import functools

import jax
import jax.numpy as jnp
import numpy as np
from jax import lax
from jax.experimental import pallas as pl
from jax.experimental.pallas import tpu as pltpu

D_MODEL = 2048
DEPTH = 2
GRID_W = 64
POOL_WINDOWS = (2, 4, 8, 16)
POOL_GROUP = D_MODEL // 8
POOL_WIDTH = POOL_GROUP * len(POOL_WINDOWS)
CONV_WIDTH = D_MODEL // 2
N_HEADS = 16
QK_NOPE = 128
QK_ROPE = 64
ROPE_AXIS = QK_ROPE // 2
V_DIM = 128
QK_DIM = QK_NOPE + QK_ROPE
Q_LORA = 512
KV_LORA = 512
ROPE_THETA = 10000.0
ATTN_SCALE = QK_DIM ** -0.5
D_FF = 5632
N_BRANCH = 3
EPS = 1e-6
OFF_A = 0
OFF_B = OFF_A + POOL_WIDTH
OFF_Q = OFF_B + 3 * CONV_WIDTH
OFF_KV = OFF_Q + Q_LORA
OFF_G = OFF_KV + KV_LORA + QK_ROPE

LANES = 128
POOL_PAD = 16
MOD_ROWS = 8
N_MOD = 6
VMEM_LIMIT = 56 * 1024 * 1024

BF16 = jnp.bfloat16
F32 = jnp.float32


def _dot(a, b):
    return jnp.dot(a, b, preferred_element_type=F32)


def _params(*sem):
    return pltpu.CompilerParams(dimension_semantics=sem, vmem_limit_bytes=VMEM_LIMIT)


def _mod_row(layer, seg, which):
    return (layer * MOD_ROWS + seg) * N_MOD + which


def _mod_kernel(cv_ref, w_ref, b_ref, o_ref):
    a = cv_ref[...]
    a = a * jax.nn.sigmoid(a)
    o_ref[...] = _dot(a.astype(BF16), w_ref[...].astype(BF16)) + b_ref[...]


def _modulation(cv, w_mod, b_mod):
    tn = 1024
    n_out = N_MOD * D_MODEL
    out = pl.pallas_call(
        _mod_kernel,
        grid=(DEPTH, n_out // tn),
        in_specs=[
            pl.BlockSpec((MOD_ROWS, D_MODEL), lambda l, j: (0, 0)),
            pl.BlockSpec((None, D_MODEL, tn), lambda l, j: (l, 0, j)),
            pl.BlockSpec((None, 1, tn), lambda l, j: (l, 0, j)),
        ],
        out_specs=pl.BlockSpec((None, MOD_ROWS, tn), lambda l, j: (l, 0, j)),
        out_shape=jax.ShapeDtypeStruct((DEPTH, MOD_ROWS, n_out), F32),
        compiler_params=_params("arbitrary", "arbitrary"),
        name="modulation",
    )(cv, w_mod, b_mod.reshape(DEPTH, 1, n_out))
    return out.reshape(DEPTH * MOD_ROWS * N_MOD, 1, D_MODEL)


def _norm_mod_kernel(x_ref, g_ref, sh_ref, sc_ref, o_ref):
    x = x_ref[...]
    y = x * lax.rsqrt(jnp.mean(x * x, axis=-1, keepdims=True) + EPS)
    y = y * g_ref[...]
    o_ref[...] = (y * (1 + sc_ref[0]) + sh_ref[0]).astype(BF16)


def _norm_mod(x, g, mods, seg_of_tile, layer, which_shift, tm):
    rows = x.shape[0]
    return pl.pallas_call(
        _norm_mod_kernel,
        grid=(rows // tm,),
        in_specs=[
            pl.BlockSpec((tm, D_MODEL), lambda i: (i, 0)),
            pl.BlockSpec((1, D_MODEL), lambda i: (0, 0)),
            pl.BlockSpec((1, 1, D_MODEL), lambda i: (_mod_row(layer, seg_of_tile(i), which_shift), 0, 0)),
            pl.BlockSpec((1, 1, D_MODEL), lambda i: (_mod_row(layer, seg_of_tile(i), which_shift + 1), 0, 0)),
        ],
        out_specs=pl.BlockSpec((tm, D_MODEL), lambda i: (i, 0)),
        out_shape=jax.ShapeDtypeStruct((rows, D_MODEL), BF16),
        compiler_params=_params("arbitrary"),
        name="norm_mod",
    )(x, g, mods, mods)


def _branch_a_kernel(h_ref, wa_ref, pw_ref, ps_ref, o_ref, ext_ref, *, seq):
    n_ext = seq + 2 * POOL_PAD
    zeros = jnp.zeros((POOL_PAD, POOL_GROUP), F32)
    ext_ref[0:POOL_PAD, :] = zeros
    ext_ref[seq + POOL_PAD:n_ext, :] = zeros
    t = lax.broadcasted_iota(jnp.int32, (seq, 1), 0)
    h = h_ref[...]
    for g, w in enumerate(POOL_WINDOWS):
        cols = slice(g * POOL_GROUP, (g + 1) * POOL_GROUP)
        a = _dot(h, wa_ref[:, cols])
        ext_ref[POOL_PAD:seq + POOL_PAD, :] = a
        e = ext_ref[...]
        c = e + pltpu.roll(e, 1, 0)
        half = 1
        while 2 * half < w:
            c = pltpu.roll(c, half, 0) + pltpu.roll(c, n_ext - half, 0)
            half *= 2
        cnt = (jnp.minimum(t + w // 2, seq) - jnp.maximum(t - w // 2, 0)).astype(F32)
        pooled = c[POOL_PAD:seq + POOL_PAD, :] / cnt - a
        mixed = _dot(pooled.astype(BF16), pw_ref[g]) * ps_ref[:, cols]
        o_ref[:, cols] = mixed.astype(BF16)


def _branch_a(h, wa, pool_w, pool_scale, seq):
    rows = h.shape[0]
    return pl.pallas_call(
        functools.partial(_branch_a_kernel, seq=seq),
        grid=(rows // seq,),
        in_specs=[
            pl.BlockSpec((seq, D_MODEL), lambda b: (b, 0)),
            pl.BlockSpec((D_MODEL, POOL_WIDTH), lambda b: (0, 0)),
            pl.BlockSpec((len(POOL_WINDOWS), POOL_GROUP, POOL_GROUP), lambda b: (0, 0, 0)),
            pl.BlockSpec((1, POOL_WIDTH), lambda b: (0, 0)),
        ],
        out_specs=pl.BlockSpec((seq, POOL_WIDTH), lambda b: (b, 0)),
        out_shape=jax.ShapeDtypeStruct((rows, POOL_WIDTH), BF16),
        scratch_shapes=[pltpu.VMEM((seq + 2 * POOL_PAD, POOL_GROUP), F32)],
        compiler_params=_params("arbitrary"),
        name="branch_a",
    )(h, wa, pool_w, pool_scale)


def _conv3(u, cw_ref, seq):
    rows = u.shape[0]
    t = lax.broadcasted_iota(jnp.int32, (rows, 1), 0) % seq
    prev = jnp.where(t == 0, 0.0, pltpu.roll(u, 1, 0))
    nxt = jnp.where(t == seq - 1, 0.0, pltpu.roll(u, rows - 1, 0))
    return prev * cw_ref[0:1, :] + u * cw_ref[1:2, :] + nxt * cw_ref[2:3, :]


def _branch_b_kernel(h_ref, wgb_ref, wgc_ref, wx_ref, cw_ref, o_ref, *, seq):
    h = h_ref[...]
    u = _dot(h, wgc_ref[...]) * _dot(h, wx_ref[...])
    o_ref[...] = (_dot(h, wgb_ref[...]) * _conv3(u, cw_ref, seq)).astype(BF16)


def _branch_b(h, wb, conv_w, seq, tm, tn):
    rows = h.shape[0]
    nb = CONV_WIDTH // tn
    return pl.pallas_call(
        functools.partial(_branch_b_kernel, seq=seq),
        grid=(rows // tm, nb),
        in_specs=[
            pl.BlockSpec((tm, D_MODEL), lambda b, c: (b, 0)),
            pl.BlockSpec((D_MODEL, tn), lambda b, c: (0, c)),
            pl.BlockSpec((D_MODEL, tn), lambda b, c: (0, nb + c)),
            pl.BlockSpec((D_MODEL, tn), lambda b, c: (0, 2 * nb + c)),
            pl.BlockSpec((3, tn), lambda b, c: (0, c)),
        ],
        out_specs=pl.BlockSpec((tm, tn), lambda b, c: (b, c)),
        out_shape=jax.ShapeDtypeStruct((rows, CONV_WIDTH), BF16),
        compiler_params=_params("arbitrary", "arbitrary"),
        name="branch_b",
    )(h, wb, wb, wb, conv_w)


def _rope_lanes(y1, r, g2_ref, cs_ref):
    u = (y1 * r * g2_ref[...]) * cs_ref[...]
    return (u + pltpu.roll(u, LANES // 2, 1))[:, :QK_ROPE]


def _lo_half_sumsq(y):
    lane = lax.broadcasted_iota(jnp.int32, (1, LANES), 1)
    return jnp.sum(jnp.where(lane < QK_ROPE, y * y, 0.0), axis=-1, keepdims=True)


def _q_proj_kernel(h_ref, wq_ref, lg_ref, wuq_ref, gn_ref, g2_ref, cs_ref, o_ref):
    zq = _dot(h_ref[...], wq_ref[...])
    cq = zq * lax.rsqrt(jnp.mean(zq * zq, axis=-1, keepdims=True) + EPS) * lg_ref[...]
    cq = cq.astype(BF16)
    for hd in range(N_HEADS):
        y = _dot(cq, wuq_ref[hd])
        y0, y1 = y[:, :QK_NOPE], y[:, QK_NOPE:]
        ss = jnp.sum(y0 * y0, axis=-1, keepdims=True) + _lo_half_sumsq(y1)
        r = lax.rsqrt(ss / QK_DIM + EPS)
        o_ref[0, hd, :, 0:QK_NOPE] = (y0 * r * gn_ref[...]).astype(BF16)
        o_ref[0, hd, :, QK_NOPE:QK_DIM] = _rope_lanes(y1, r, g2_ref, cs_ref).astype(BF16)


def _q_proj(h, wq, lora_g, wuq, gn, g2, cs, batch, seq, tm):
    nt = seq // tm
    return pl.pallas_call(
        _q_proj_kernel,
        grid=(batch * nt,),
        in_specs=[
            pl.BlockSpec((tm, D_MODEL), lambda i: (i, 0)),
            pl.BlockSpec((D_MODEL, Q_LORA), lambda i: (0, 0)),
            pl.BlockSpec((1, Q_LORA), lambda i: (0, 0)),
            pl.BlockSpec((N_HEADS, Q_LORA, 2 * LANES), lambda i: (0, 0, 0)),
            pl.BlockSpec((1, LANES), lambda i: (0, 0)),
            pl.BlockSpec((1, LANES), lambda i: (0, 0)),
            pl.BlockSpec((tm, LANES), lambda i: (i % nt, 0)),
        ],
        out_specs=pl.BlockSpec((1, N_HEADS, tm, QK_DIM), lambda i: (i // nt, 0, i % nt, 0)),
        out_shape=jax.ShapeDtypeStruct((batch, N_HEADS, seq, QK_DIM), BF16),
        compiler_params=_params("arbitrary"),
        name="q_proj",
    )(h, wq, lora_g, wuq, gn, g2, cs)


def _kv_proj_kernel(h_ref, wkv_ref, lg_ref, wukv_ref, gn_ref, g2_ref, cs_ref, k_ref, v_ref):
    z = _dot(h_ref[...], wkv_ref[...])
    zc, kr = z[:, :KV_LORA], z[:, KV_LORA:]
    ckv = zc * lax.rsqrt(jnp.mean(zc * zc, axis=-1, keepdims=True) + EPS) * lg_ref[...]
    ckv = ckv.astype(BF16)
    ss_rope = _lo_half_sumsq(kr)
    for hd in range(N_HEADS):
        kv = _dot(ckv, wukv_ref[:, hd * 2 * LANES:(hd + 1) * 2 * LANES])
        kn, v = kv[:, :QK_NOPE], kv[:, QK_NOPE:]
        ss = jnp.sum(kn * kn, axis=-1, keepdims=True) + ss_rope
        r = lax.rsqrt(ss / QK_DIM + EPS)
        k_ref[0, hd, :, 0:QK_NOPE] = (kn * r * gn_ref[...]).astype(BF16)
        k_ref[0, hd, :, QK_NOPE:QK_DIM] = _rope_lanes(kr, r, g2_ref, cs_ref).astype(BF16)
        v_ref[0, hd] = v.astype(BF16)


def _kv_proj(h, wkv, lora_g, wukv, gn, g2, cs, batch, seq, tm):
    nt = seq // tm
    return pl.pallas_call(
        _kv_proj_kernel,
        grid=(batch * nt,),
        in_specs=[
            pl.BlockSpec((tm, D_MODEL), lambda i: (i, 0)),
            pl.BlockSpec((D_MODEL, KV_LORA + LANES), lambda i: (0, 0)),
            pl.BlockSpec((1, KV_LORA), lambda i: (0, 0)),
            pl.BlockSpec((KV_LORA, N_HEADS * 2 * LANES), lambda i: (0, 0)),
            pl.BlockSpec((1, LANES), lambda i: (0, 0)),
            pl.BlockSpec((1, LANES), lambda i: (0, 0)),
            pl.BlockSpec((tm, LANES), lambda i: (i % nt, 0)),
        ],
        out_specs=[
            pl.BlockSpec((1, N_HEADS, tm, QK_DIM), lambda i: (i // nt, 0, i % nt, 0)),
            pl.BlockSpec((1, N_HEADS, tm, V_DIM), lambda i: (i // nt, 0, i % nt, 0)),
        ],
        out_shape=[
            jax.ShapeDtypeStruct((batch, N_HEADS, seq, QK_DIM), BF16),
            jax.ShapeDtypeStruct((batch, N_HEADS, seq, V_DIM), BF16),
        ],
        compiler_params=_params("arbitrary"),
        name="kv_proj",
    )(h, wkv, lora_g, wukv, gn, g2, cs)


def _attn_kernel(q_ref, *refs):
    o_ref = refs[-1]
    kv_refs = refs[:-1]
    q = q_ref[0, 0]
    scores = []
    for s in range(0, len(kv_refs), 2):
        k = kv_refs[s][0, 0]
        sc = lax.dot_general(q, k, (((1,), (1,)), ((), ())), preferred_element_type=F32)
        scores.append(sc * ATTN_SCALE)
    m = functools.reduce(jnp.maximum, [jnp.max(sc, axis=-1, keepdims=True) for sc in scores])
    denom = 0.0
    acc = 0.0
    for s, sc in enumerate(scores):
        p = jnp.exp(sc - m)
        denom = denom + jnp.sum(p, axis=-1, keepdims=True)
        acc = acc + _dot(p.astype(BF16), kv_refs[2 * s + 1][0, 0])
    o_ref[0] = (acc / denom).astype(BF16)


def _attention(q, kvs, tq):
    batch, _, seq, _ = q.shape
    in_specs = [pl.BlockSpec((1, 1, tq, QK_DIM), lambda b, hd, i: (b, hd, i, 0))]
    args = [q]
    for k, v in kvs:
        in_specs.append(pl.BlockSpec((1, 1, k.shape[2], QK_DIM), lambda b, hd, i: (b, hd, 0, 0)))
        in_specs.append(pl.BlockSpec((1, 1, v.shape[2], V_DIM), lambda b, hd, i: (b, hd, 0, 0)))
        args += [k, v]
    return pl.pallas_call(
        _attn_kernel,
        grid=(batch, N_HEADS, seq // tq),
        in_specs=in_specs,
        out_specs=pl.BlockSpec((1, tq, V_DIM), lambda b, hd, i: (b, i, hd)),
        out_shape=jax.ShapeDtypeStruct((batch, seq, N_HEADS * V_DIM), BF16),
        compiler_params=_params("arbitrary", "arbitrary", "arbitrary"),
        name="attention",
    )(*args)


def _merge_kernel(h_ref, pa_ref, pb_ref, oc_ref, wg0_ref, wg1_ref, wg2_ref, wa_ref, wb_ref, wc_ref, o_ref):
    h = h_ref[...]
    merged = jax.nn.sigmoid(_dot(h, wg0_ref[...])) * _dot(pa_ref[...], wa_ref[...])
    merged = merged + jax.nn.sigmoid(_dot(h, wg1_ref[...])) * _dot(pb_ref[...], wb_ref[...])
    merged = merged + jax.nn.sigmoid(_dot(h, wg2_ref[...])) * _dot(oc_ref[...], wc_ref[...])
    o_ref[...] = merged.astype(BF16)


def _merge(h, pa, pb, oc, wg, wba, wbb, wbc, tm, tn):
    rows = h.shape[0]
    nb = D_MODEL // tn
    row_spec = lambda width: pl.BlockSpec((tm, width), lambda i, j: (i, 0))
    col_spec = lambda depth, off: pl.BlockSpec((depth, tn), lambda i, j: (0, off + j))
    return pl.pallas_call(
        _merge_kernel,
        grid=(rows // tm, nb),
        in_specs=[
            row_spec(D_MODEL), row_spec(POOL_WIDTH), row_spec(CONV_WIDTH), row_spec(N_HEADS * V_DIM),
            col_spec(D_MODEL, 0), col_spec(D_MODEL, nb), col_spec(D_MODEL, 2 * nb),
            col_spec(POOL_WIDTH, 0), col_spec(CONV_WIDTH, 0), col_spec(N_HEADS * V_DIM, 0),
        ],
        out_specs=pl.BlockSpec((tm, tn), lambda i, j: (i, j)),
        out_shape=jax.ShapeDtypeStruct((rows, D_MODEL), BF16),
        compiler_params=_params("arbitrary", "arbitrary"),
        name="merge",
    )(h, pa, pb, oc, wg, wg, wg, wba, wbb, wbc)


def _matmul_residual_kernel(a_ref, w_ref, res_ref, gate_ref, o_ref):
    o_ref[...] = res_ref[...] + gate_ref[0] * _dot(a_ref[...], w_ref[...])


def _matmul_residual(a, w, res, mods, seg_of_tile, layer, which_gate, tm, tn):
    rows, depth = a.shape
    return pl.pallas_call(
        _matmul_residual_kernel,
        grid=(rows // tm, D_MODEL // tn),
        in_specs=[
            pl.BlockSpec((tm, depth), lambda i, j: (i, 0)),
            pl.BlockSpec((depth, tn), lambda i, j: (0, j)),
            pl.BlockSpec((tm, tn), lambda i, j: (i, j)),
            pl.BlockSpec((1, 1, tn), lambda i, j: (_mod_row(layer, seg_of_tile(i), which_gate), 0, j)),
        ],
        out_specs=pl.BlockSpec((tm, tn), lambda i, j: (i, j)),
        out_shape=jax.ShapeDtypeStruct((rows, D_MODEL), F32),
        compiler_params=_params("arbitrary", "arbitrary"),
        name="matmul_residual",
    )(a, w, res, mods)


def _ffn_up_kernel(h_ref, wu_ref, wv_ref, cw_ref, o_ref, *, seq):
    h = h_ref[...]
    cu = _conv3(_dot(h, wu_ref[...]), cw_ref, seq)
    o_ref[...] = (cu * jax.nn.sigmoid(cu) * _dot(h, wv_ref[...])).astype(BF16)


def _ffn_up(h, w_up, conv, seq, tm, tn):
    rows = h.shape[0]
    nb = D_FF // tn
    return pl.pallas_call(
        functools.partial(_ffn_up_kernel, seq=seq),
        grid=(rows // tm, nb),
        in_specs=[
            pl.BlockSpec((tm, D_MODEL), lambda b, f: (b, 0)),
            pl.BlockSpec((D_MODEL, tn), lambda b, f: (0, f)),
            pl.BlockSpec((D_MODEL, tn), lambda b, f: (0, nb + f)),
            pl.BlockSpec((3, tn), lambda b, f: (0, f)),
        ],
        out_specs=pl.BlockSpec((tm, tn), lambda b, f: (b, f)),
        out_shape=jax.ShapeDtypeStruct((rows, D_FF), BF16),
        compiler_params=_params("arbitrary", "arbitrary"),
        name="ffn_up",
    )(h, w_up, w_up, conv)


_ROPE_SWAP = np.concatenate([np.arange(16, 32), np.arange(0, 16), np.arange(48, 64), np.arange(32, 48)])


def _rope_table(length):
    rows = length // GRID_W
    row = jnp.repeat(jnp.arange(rows, dtype=jnp.int32), GRID_W).astype(F32)
    col = jnp.tile(jnp.arange(GRID_W, dtype=jnp.int32), rows).astype(F32)
    inv = ROPE_THETA ** (-jnp.arange(0, ROPE_AXIS, 2, dtype=F32) / ROPE_AXIS)
    ang_r = row[:, None] * inv[None]
    ang_c = col[:, None] * inv[None]
    cr, sr, cc, sc = jnp.cos(ang_r), jnp.sin(ang_r), jnp.cos(ang_c), jnp.sin(ang_c)
    return jnp.concatenate([cr, cr, cc, cc, -sr, sr, -sc, sc], axis=-1)


def _identity_table(length):
    return jnp.concatenate([jnp.ones((length, QK_ROPE), F32), jnp.zeros((length, QK_ROPE), F32)], axis=-1)


def _head_gains(g):
    rope = g[QK_NOPE:]
    return g[:QK_NOPE][None], jnp.concatenate([rope, rope[_ROPE_SWAP]])[None]


def _layer_weights(i, w_in, pool_w, pool_scale, conv_w, q_lora_g, w_uq, kv_lora_g, w_ukv, q_head_g,
                   k_head_g, w_branch_a, w_branch_b, w_branch_c, w_out, w_ffn_up, ffn_conv, w_ffn_down):
    wi = w_in[i]
    k_rope = wi[:, OFF_KV + KV_LORA:OFF_G]
    uq = jnp.transpose(w_uq[i], (1, 0, 2))
    q_gn, q_g2 = _head_gains(q_head_g[i])
    k_gn, k_g2 = _head_gains(k_head_g[i])
    return dict(
        wa=wi[:, OFF_A:OFF_B].astype(BF16),
        wb=wi[:, OFF_B:OFF_Q].astype(BF16),
        wq=wi[:, OFF_Q:OFF_KV].astype(BF16),
        wkv=jnp.concatenate([wi[:, OFF_KV:OFF_KV + KV_LORA], k_rope, k_rope[:, _ROPE_SWAP]], axis=1).astype(BF16),
        wg=wi[:, OFF_G:].astype(BF16),
        pool_w=pool_w[i].astype(BF16), pool_scale=pool_scale[i][None], conv_w=conv_w[i],
        q_lora_g=q_lora_g[i][None], kv_lora_g=kv_lora_g[i][None],
        wuq=jnp.concatenate([uq, uq[:, :, QK_NOPE:][:, :, _ROPE_SWAP]], axis=2).astype(BF16),
        wukv=w_ukv[i].reshape(KV_LORA, N_HEADS * (QK_NOPE + V_DIM)).astype(BF16),
        q_gn=q_gn, q_g2=q_g2, k_gn=k_gn, k_g2=k_g2,
        wba=w_branch_a[i].astype(BF16), wbb=w_branch_b[i].astype(BF16), wbc=w_branch_c[i].astype(BF16),
        w_out=w_out[i].astype(BF16), w_up=w_ffn_up[i].astype(BF16), ffn_conv=ffn_conv[i],
        w_down=w_ffn_down[i].astype(BF16),
    )


def _mixers(h, lp, cs, batch, seq, ctx_kv, tiles):
    pa = _branch_a(h, lp['wa'], lp['pool_w'], lp['pool_scale'], seq)
    pb = _branch_b(h, lp['wb'], lp['conv_w'], seq, tiles['seq_tm'], 256)
    q = _q_proj(h, lp['wq'], lp['q_lora_g'], lp['wuq'], lp['q_gn'], lp['q_g2'], cs, batch, seq, tiles['proj_tm'])
    k, v = _kv_proj(h, lp['wkv'], lp['kv_lora_g'], lp['wukv'], lp['k_gn'], lp['k_g2'], cs, batch, seq,
                    tiles['proj_tm'])
    kvs = [(k, v)] if ctx_kv is None else [(k, v), ctx_kv]
    oc = _attention(q, kvs, tiles['tq']).reshape(batch * seq, N_HEADS * V_DIM)
    merged = _merge(h, pa, pb, oc, lp['wg'], lp['wba'], lp['wbb'], lp['wbc'], tiles['tm'], 256)
    return merged, (k, v)


def _layer_stream(x, lp, mods, norm1_g, norm2_g, layer, seg_of_tile, cs, batch, seq, ctx_kv, tiles):
    tm = tiles['tm']
    h = _norm_mod(x, norm1_g, mods, seg_of_tile, layer, 0, tm)
    merged, kv = _mixers(h, lp, cs, batch, seq, ctx_kv, tiles)
    x = _matmul_residual(merged, lp['w_out'], x, mods, seg_of_tile, layer, 2, tm, 512)
    h2 = _norm_mod(x, norm2_g, mods, seg_of_tile, layer, 3, tm)
    hidden = _ffn_up(h2, lp['w_up'], lp['ffn_conv'], seq, tiles['seq_tm'], 256)
    x = _matmul_residual(hidden, lp['w_down'], x, mods, seg_of_tile, layer, 5, tm, 512)
    return x, kv


def kernel(x, c, ctx, c_ctx, norm1_g, norm2_g, w_mod, b_mod, w_in, pool_w, pool_scale, conv_w, q_lora_g, w_uq,
           kv_lora_g, w_ukv, q_head_g, k_head_g, w_branch_a, w_branch_b, w_branch_c, w_out, w_ffn_up, ffn_conv,
           w_ffn_down):
    batch, seq, _ = x.shape
    ctx_len = ctx.shape[1]
    assert batch + 1 <= MOD_ROWS
    cv = jnp.concatenate([c, c_ctx[None], jnp.zeros((MOD_ROWS - batch - 1, D_MODEL), F32)], axis=0)
    mods = _modulation(cv, w_mod, b_mod)

    cs_x = _rope_table(seq)
    cs_c = _identity_table(ctx_len)
    x_tiles = dict(tm=512, seq_tm=seq, proj_tm=512, tq=512)
    c_tiles = dict(tm=256, seq_tm=batch * ctx_len, proj_tm=ctx_len, tq=ctx_len)
    x_seg = lambda i: i // (seq // x_tiles['tm'])
    c_seg = lambda i: batch

    xs = x.reshape(batch * seq, D_MODEL)
    cx = ctx.reshape(batch * ctx_len, D_MODEL)
    for i in range(DEPTH):
        lp = _layer_weights(i, w_in, pool_w, pool_scale, conv_w, q_lora_g, w_uq, kv_lora_g, w_ukv, q_head_g,
                            k_head_g, w_branch_a, w_branch_b, w_branch_c, w_out, w_ffn_up, ffn_conv, w_ffn_down)
        n1, n2 = norm1_g[i][None], norm2_g[i][None]
        if i == DEPTH - 1:
            hc = _norm_mod(cx, n1, mods, c_seg, i, 0, c_tiles['tm'])
            ctx_kv = _kv_proj(hc, lp['wkv'], lp['kv_lora_g'], lp['wukv'], lp['k_gn'], lp['k_g2'], cs_c, batch,
                              ctx_len, c_tiles['proj_tm'])
        else:
            cx, ctx_kv = _layer_stream(cx, lp, mods, n1, n2, i, c_seg, cs_c, batch, ctx_len, None, c_tiles)
        xs, _ = _layer_stream(xs, lp, mods, n1, n2, i, x_seg, cs_x, batch, seq, tuple(ctx_kv), x_tiles)
    return xs.reshape(batch, seq, D_MODEL)
```

```python
import functools
import math

import jax
import jax.numpy as jnp
import numpy as np
from jax import lax
from jax.experimental import pallas as pl
from jax.experimental.pallas import tpu as pltpu

D_MODEL = 2048
DEPTH = 2
GRID_W = 64
POOL_WINDOWS = (2, 4, 8, 16)
POOL_GROUP = D_MODEL // 8
POOL_WIDTH = POOL_GROUP * len(POOL_WINDOWS)
CONV_WIDTH = D_MODEL // 2
N_HEADS = 16
QK_NOPE = 128
QK_ROPE = 64
ROPE_AXIS = QK_ROPE // 2
V_DIM = 128
QK_DIM = QK_NOPE + QK_ROPE
Q_LORA = 512
KV_LORA = 512
ROPE_THETA = 10000.0
ATTN_SCALE = QK_DIM ** -0.5
D_FF = 5632
N_BRANCH = 3
EPS = 1e-6
OFF_A = 0
OFF_B = OFF_A + POOL_WIDTH
OFF_Q = OFF_B + 3 * CONV_WIDTH
OFF_KV = OFF_Q + Q_LORA
OFF_KROPE = OFF_KV + KV_LORA
OFF_G = OFF_KROPE + QK_ROPE

LANES = 128
QK_STORE = 2 * LANES
MXU_ROWS = 512
POOL_PAD = 16
MOD_ROWS = 8
N_MOD = 6
VMEM_LIMIT = 56 * 1024 * 1024
KEY_CHUNK = 512

BF16 = jnp.bfloat16
F32 = jnp.float32


def _dot(a, b):
    return jnp.dot(a, b, preferred_element_type=F32)


def _dot_rows(a_ref, w):
    rows = a_ref.shape[0]
    if rows <= MXU_ROWS:
        return _dot(a_ref[...], w)
    return jnp.concatenate([_dot(a_ref[r:r + MXU_ROWS, :], w) for r in range(0, rows, MXU_ROWS)], axis=0)


def _params(*sem):
    return pltpu.CompilerParams(dimension_semantics=sem, vmem_limit_bytes=VMEM_LIMIT)


def _mod_row(layer, seg, which):
    return (layer * MOD_ROWS + seg) * N_MOD + which


def _mod_kernel(cv_ref, w_ref, b_ref, o_ref):
    a = cv_ref[...]
    a = a * jax.nn.sigmoid(a)
    o_ref[...] = _dot(a.astype(BF16), w_ref[...].astype(BF16)) + b_ref[...]


def _modulation(cv, w_mod, b_mod):
    tn = 1024
    n_out = N_MOD * D_MODEL
    out = pl.pallas_call(
        _mod_kernel,
        grid=(DEPTH, n_out // tn),
        in_specs=[
            pl.BlockSpec((MOD_ROWS, D_MODEL), lambda l, j: (0, 0)),
            pl.BlockSpec((None, D_MODEL, tn), lambda l, j: (l, 0, j)),
            pl.BlockSpec((None, 1, tn), lambda l, j: (l, 0, j)),
        ],
        out_specs=pl.BlockSpec((None, MOD_ROWS, tn), lambda l, j: (l, 0, j)),
        out_shape=jax.ShapeDtypeStruct((DEPTH, MOD_ROWS, n_out), F32),
        compiler_params=_params("arbitrary", "arbitrary"),
        name="modulation",
    )(cv, w_mod, b_mod.reshape(DEPTH, 1, n_out))
    return out.reshape(DEPTH * MOD_ROWS * N_MOD, 1, D_MODEL)


def _norm_mod_kernel(x_ref, g_ref, sh_ref, sc_ref, o_ref):
    x = x_ref[...]
    y = x * lax.rsqrt(jnp.mean(x * x, axis=-1, keepdims=True) + EPS)
    y = y * g_ref[0]
    o_ref[...] = (y * (1 + sc_ref[0]) + sh_ref[0]).astype(BF16)


def _norm_mod(x, g, mods, seg_of_tile, layer, which_shift, tm):
    rows = x.shape[0]
    return pl.pallas_call(
        _norm_mod_kernel,
        grid=(rows // tm,),
        in_specs=[
            pl.BlockSpec((tm, D_MODEL), lambda i: (i, 0)),
            pl.BlockSpec((1, 1, D_MODEL), lambda i: (layer, 0, 0)),
            pl.BlockSpec((1, 1, D_MODEL), lambda i: (_mod_row(layer, seg_of_tile(i), which_shift), 0, 0)),
            pl.BlockSpec((1, 1, D_MODEL), lambda i: (_mod_row(layer, seg_of_tile(i), which_shift + 1), 0, 0)),
        ],
        out_specs=pl.BlockSpec((tm, D_MODEL), lambda i: (i, 0)),
        out_shape=jax.ShapeDtypeStruct((rows, D_MODEL), BF16),
        compiler_params=_params("arbitrary"),
        name="norm_mod",
    )(x, g.reshape(DEPTH, 1, D_MODEL), mods, mods)


def _branch_a_kernel(h_ref, wa_ref, pw_ref, ps_ref, o_ref, ext_ref, *, seq):
    n_ext = seq + 2 * POOL_PAD
    zeros = jnp.zeros((POOL_PAD, POOL_GROUP), F32)
    ext_ref[0:POOL_PAD, :] = zeros
    ext_ref[seq + POOL_PAD:n_ext, :] = zeros
    t = lax.broadcasted_iota(jnp.int32, (seq, 1), 0)
    for g, w in enumerate(POOL_WINDOWS):
        cols = slice(g * POOL_GROUP, (g + 1) * POOL_GROUP)
        a = _dot_rows(h_ref, wa_ref[:, cols])
        ext_ref[POOL_PAD:seq + POOL_PAD, :] = a
        e = ext_ref[...]
        c = e + pltpu.roll(e, 1, 0)
        half = 1
        while 2 * half < w:
            c = pltpu.roll(c, half, 0) + pltpu.roll(c, n_ext - half, 0)
            half *= 2
        cnt = (jnp.minimum(t + w // 2, seq) - jnp.maximum(t - w // 2, 0)).astype(F32)
        pooled = c[POOL_PAD:seq + POOL_PAD, :] / cnt - a
        mixed = _dot(pooled.astype(BF16), pw_ref[g].astype(BF16)) * ps_ref[0, :, cols]
        o_ref[:, cols] = mixed.astype(BF16)


def _branch_a(h, wa, pool_w, pool_scale, layer, seq):
    rows = h.shape[0]
    n_win = len(POOL_WINDOWS)
    return pl.pallas_call(
        functools.partial(_branch_a_kernel, seq=seq),
        grid=(rows // seq,),
        in_specs=[
            pl.BlockSpec((seq, D_MODEL), lambda b: (b, 0)),
            pl.BlockSpec((None, D_MODEL, POOL_WIDTH), lambda b: (layer, 0, 0)),
            pl.BlockSpec((None, n_win, POOL_GROUP, POOL_GROUP), lambda b: (layer, 0, 0, 0)),
            pl.BlockSpec((1, 1, POOL_WIDTH), lambda b: (layer, 0, 0)),
        ],
        out_specs=pl.BlockSpec((seq, POOL_WIDTH), lambda b: (b, 0)),
        out_shape=jax.ShapeDtypeStruct((rows, POOL_WIDTH), BF16),
        scratch_shapes=[pltpu.VMEM((seq + 2 * POOL_PAD, POOL_GROUP), F32)],
        compiler_params=_params("arbitrary"),
        name="branch_a",
    )(h, wa, pool_w, pool_scale.reshape(DEPTH, 1, POOL_WIDTH))


def _conv3(u, cw_ref, seq):
    rows = u.shape[0]
    t = lax.broadcasted_iota(jnp.int32, (rows, 1), 0) % seq
    prev = jnp.where(t == 0, 0.0, pltpu.roll(u, 1, 0))
    nxt = jnp.where(t == seq - 1, 0.0, pltpu.roll(u, rows - 1, 0))
    return prev * cw_ref[0:1, :] + u * cw_ref[1:2, :] + nxt * cw_ref[2:3, :]


def _branch_b_kernel(h_ref, wgb_ref, wgc_ref, wx_ref, cw_ref, o_ref, *, seq):
    u = _dot_rows(h_ref, wgc_ref[...].astype(BF16)) * _dot_rows(h_ref, wx_ref[...].astype(BF16))
    gate = _dot_rows(h_ref, wgb_ref[...].astype(BF16))
    o_ref[...] = (gate * _conv3(u, cw_ref, seq)).astype(BF16)


def _branch_b(h, w_in, conv_w, layer, seq, tm, tn):
    rows = h.shape[0]
    nb = CONV_WIDTH // tn
    first = OFF_B // tn
    w_spec = lambda k: pl.BlockSpec((None, D_MODEL, tn), lambda b, c: (layer, 0, first + k * nb + c))
    return pl.pallas_call(
        functools.partial(_branch_b_kernel, seq=seq),
        grid=(rows // tm, nb),
        in_specs=[
            pl.BlockSpec((tm, D_MODEL), lambda b, c: (b, 0)),
            w_spec(0), w_spec(1), w_spec(2),
            pl.BlockSpec((None, 3, tn), lambda b, c: (layer, 0, c)),
        ],
        out_specs=pl.BlockSpec((tm, tn), lambda b, c: (b, c)),
        out_shape=jax.ShapeDtypeStruct((rows, CONV_WIDTH), BF16),
        compiler_params=_params("arbitrary", "arbitrary"),
        name="branch_b",
    )(h, w_in, w_in, w_in, conv_w)


def _lo_lanes():
    return lax.broadcasted_iota(jnp.int32, (1, LANES), 1) < QK_ROPE


def _q_proj_kernel(h_ref, wq_ref, lg_ref, wuq_ref, gn_ref, g2_ref, cs_ref, o_ref):
    zq = _dot(h_ref[...], wq_ref[...].astype(BF16))
    cq = zq * lax.rsqrt(jnp.mean(zq * zq, axis=-1, keepdims=True) + EPS) * lg_ref[0]
    cq = cq.astype(BF16)
    gain_table = g2_ref[0] * cs_ref[...]
    lo = _lo_lanes()
    for pair in range(N_HEADS // 2):
        y = _dot(cq, wuq_ref[pair])
        nope = (y[:, 0:LANES], y[:, LANES:2 * LANES])
        yr, ys = y[:, 2 * LANES:3 * LANES], y[:, 3 * LANES:]
        sq_r = yr * yr
        rot = yr * gain_table[:, :LANES] + ys * gain_table[:, LANES:]
        for par in range(2):
            sq = nope[par] * nope[par] + jnp.where(lo if par == 0 else ~lo, sq_r, 0.0)
            r = lax.rsqrt(jnp.sum(sq, axis=-1, keepdims=True) / QK_DIM + EPS)
            hd = 2 * pair + par
            o_ref[0, hd, :, 0:LANES] = (nope[par] * r * gn_ref[0]).astype(BF16)
            o_ref[0, hd, :, LANES:QK_STORE] = jnp.where(lo if par == 0 else ~lo, rot * r, 0.0).astype(BF16)


def _q_proj(h, w_in, lora_g, wuq, gn, g2, cs, layer, batch, seq, tm):
    nt = seq // tm
    vec = lambda width: pl.BlockSpec((1, 1, width), lambda i: (layer, 0, 0))
    return pl.pallas_call(
        _q_proj_kernel,
        grid=(batch * nt,),
        in_specs=[
            pl.BlockSpec((tm, D_MODEL), lambda i: (i, 0)),
            pl.BlockSpec((None, D_MODEL, Q_LORA), lambda i: (layer, 0, OFF_Q // Q_LORA)),
            vec(Q_LORA),
            pl.BlockSpec((None, N_HEADS // 2, Q_LORA, 4 * LANES), lambda i: (layer, 0, 0, 0)),
            vec(LANES), vec(2 * LANES),
            pl.BlockSpec((tm, 2 * LANES), lambda i: (i % nt, 0)),
        ],
        out_specs=pl.BlockSpec((1, N_HEADS, tm, QK_STORE), lambda i: (i // nt, 0, i % nt, 0)),
        out_shape=jax.ShapeDtypeStruct((batch, N_HEADS, seq, QK_STORE), BF16),
        compiler_params=_params("arbitrary"),
        name="q_proj",
    )(h, w_in, lora_g.reshape(DEPTH, 1, Q_LORA), wuq, gn, g2, cs)


def _kv_proj_kernel(h_ref, wkv_ref, wkr_ref, lg_ref, wukv_ref, gn_ref, g2_ref, cs_ref, k_ref, v_ref):
    h = h_ref[...]
    zc = _dot(h, wkv_ref[...].astype(BF16))
    kr = _dot(h, wkr_ref[...])
    ckv = zc * lax.rsqrt(jnp.mean(zc * zc, axis=-1, keepdims=True) + EPS) * lg_ref[0]
    ckv = ckv.astype(BF16)
    lo = _lo_lanes()
    ss_rope = jnp.sum(jnp.where(lo, kr * kr, 0.0), axis=-1, keepdims=True)
    u = kr * (g2_ref[0] * cs_ref[...])
    rot = u + pltpu.roll(u, LANES // 2, 1)
    rope = (jnp.where(lo, rot, 0.0), jnp.where(lo, 0.0, rot))
    ones = jnp.ones((h.shape[0], LANES), BF16)
    for hd in range(N_HEADS):
        kv = _dot(ckv, wukv_ref[:, hd * 2 * LANES:(hd + 1) * 2 * LANES])
        kn, v = kv[:, :QK_NOPE], kv[:, QK_NOPE:]
        ss = jnp.sum(kn * kn, axis=-1, keepdims=True) + ss_rope
        r = lax.rsqrt(ss / QK_DIM + EPS)
        k_ref[0, hd, :, 0:LANES] = (kn * r * gn_ref[0]).astype(BF16)
        k_ref[0, hd, :, LANES:QK_STORE] = (rope[hd % 2] * r).astype(BF16)
        v_ref[0, hd, :, 0:V_DIM] = v.astype(BF16)
        v_ref[0, hd, :, V_DIM:2 * V_DIM] = ones


def _kv_proj(h, w_in, wkr, lora_g, wukv, gn, g2, cs, layer, batch, seq, tm):
    nt = seq // tm
    vec = lambda width: pl.BlockSpec((1, 1, width), lambda i: (layer, 0, 0))
    return pl.pallas_call(
        _kv_proj_kernel,
        grid=(batch * nt,),
        in_specs=[
            pl.BlockSpec((tm, D_MODEL), lambda i: (i, 0)),
            pl.BlockSpec((None, D_MODEL, KV_LORA), lambda i: (layer, 0, OFF_KV // KV_LORA)),
            pl.BlockSpec((None, D_MODEL, LANES), lambda i: (layer, 0, 0)),
            vec(KV_LORA),
            pl.BlockSpec((None, KV_LORA, N_HEADS * 2 * LANES), lambda i: (layer, 0, 0)),
            vec(LANES), vec(LANES),
            pl.BlockSpec((tm, LANES), lambda i: (i % nt, 0)),
        ],
        out_specs=[
            pl.BlockSpec((1, N_HEADS, tm, QK_STORE), lambda i: (i // nt, 0, i % nt, 0)),
            pl.BlockSpec((1, N_HEADS, tm, 2 * V_DIM), lambda i: (i // nt, 0, i % nt, 0)),
        ],
        out_shape=[
            jax.ShapeDtypeStruct((batch, N_HEADS, seq, QK_STORE), BF16),
            jax.ShapeDtypeStruct((batch, N_HEADS, seq, 2 * V_DIM), BF16),
        ],
        compiler_params=_params("arbitrary"),
        name="kv_proj",
    )(h, w_in, wkr, lora_g.reshape(DEPTH, 1, KV_LORA), wukv, gn, g2, cs)


def _attn_kernel(q_ref, *refs):
    o_ref = refs[-1]
    kv_refs = refs[:-1]
    q = q_ref[0, 0]
    scale = ATTN_SCALE * math.log2(math.e)
    m = acc = None
    for s in range(0, len(kv_refs), 2):
        k_ref, v_ref = kv_refs[s], kv_refs[s + 1]
        n_keys = k_ref.shape[2]
        tk = min(KEY_CHUNK, n_keys)
        for j in range(n_keys // tk):
            k = k_ref[0, 0, j * tk:(j + 1) * tk, :]
            v = v_ref[0, 0, j * tk:(j + 1) * tk, :]
            sc = lax.dot_general(q, k, (((1,), (1,)), ((), ())), preferred_element_type=F32)
            m_chunk = jnp.max(sc, axis=-1, keepdims=True)
            if m is None:
                m = m_chunk
                acc = _dot(jnp.exp2((sc - m) * scale).astype(BF16), v)
            else:
                m_new = jnp.maximum(m, m_chunk)
                alpha = jnp.exp2((m - m_new) * scale)
                acc = alpha * acc + _dot(jnp.exp2((sc - m_new) * scale).astype(BF16), v)
                m = m_new
    o_ref[0] = (acc[:, :V_DIM] / acc[:, V_DIM:]).astype(BF16)


def _attention(q, kvs, tq):
    batch, _, seq, _ = q.shape
    in_specs = [pl.BlockSpec((1, 1, tq, QK_STORE), lambda b, hd, i: (b, hd, i, 0))]
    args = [q]
    for k, v in kvs:
        in_specs.append(pl.BlockSpec((1, 1, k.shape[2], QK_STORE), lambda b, hd, i: (b, hd, 0, 0)))
        in_specs.append(pl.BlockSpec((1, 1, v.shape[2], 2 * V_DIM), lambda b, hd, i: (b, hd, 0, 0)))
        args += [k, v]
    return pl.pallas_call(
        _attn_kernel,
        grid=(batch, N_HEADS, seq // tq),
        in_specs=in_specs,
        out_specs=pl.BlockSpec((1, tq, V_DIM), lambda b, hd, i: (b, i, hd)),
        out_shape=jax.ShapeDtypeStruct((batch, seq, N_HEADS * V_DIM), BF16),
        compiler_params=_params("arbitrary", "arbitrary", "arbitrary"),
        name="attention",
    )(*args)


def _merge_kernel(h_ref, pa_ref, pb_ref, oc_ref, wg0_ref, wg1_ref, wg2_ref, wa_ref, wb_ref, wc_ref, o_ref):
    gated = lambda wg_ref, y_ref, w_ref: (jax.nn.sigmoid(_dot_rows(h_ref, wg_ref[...]))
                                          * _dot_rows(y_ref, w_ref[...].astype(BF16)))
    merged = gated(wg0_ref, pa_ref, wa_ref)
    merged = merged + gated(wg1_ref, pb_ref, wb_ref)
    merged = merged + gated(wg2_ref, oc_ref, wc_ref)
    o_ref[...] = merged.astype(BF16)


def _merge(h, pa, pb, oc, wg, wba, wbb, wbc, layer, tm, tn):
    rows = h.shape[0]
    nb = D_MODEL // tn
    row_spec = lambda width: pl.BlockSpec((tm, width), lambda i, j: (i, 0))
    col_spec = lambda depth, off: pl.BlockSpec((None, depth, tn), lambda i, j: (layer, 0, off + j))
    return pl.pallas_call(
        _merge_kernel,
        grid=(rows // tm, nb),
        in_specs=[
            row_spec(D_MODEL), row_spec(POOL_WIDTH), row_spec(CONV_WIDTH), row_spec(N_HEADS * V_DIM),
            col_spec(D_MODEL, 0), col_spec(D_MODEL, nb), col_spec(D_MODEL, 2 * nb),
            col_spec(POOL_WIDTH, 0), col_spec(CONV_WIDTH, 0), col_spec(N_HEADS * V_DIM, 0),
        ],
        out_specs=pl.BlockSpec((tm, tn), lambda i, j: (i, j)),
        out_shape=jax.ShapeDtypeStruct((rows, D_MODEL), BF16),
        compiler_params=_params("arbitrary", "arbitrary"),
        name="merge",
    )(h, pa, pb, oc, wg, wg, wg, wba, wbb, wbc)


def _matmul_residual_kernel(a_ref, w_ref, res_ref, gate_ref, o_ref):
    o_ref[...] = res_ref[...] + gate_ref[0] * _dot_rows(a_ref, w_ref[...].astype(BF16))


def _matmul_residual(a, w, res, mods, seg_of_tile, layer, which_gate, tm, tn):
    rows, depth = a.shape
    return pl.pallas_call(
        _matmul_residual_kernel,
        grid=(rows // tm, D_MODEL // tn),
        in_specs=[
            pl.BlockSpec((tm, depth), lambda i, j: (i, 0)),
            pl.BlockSpec((None, depth, tn), lambda i, j: (layer, 0, j)),
            pl.BlockSpec((tm, tn), lambda i, j: (i, j)),
            pl.BlockSpec((1, 1, tn), lambda i, j: (_mod_row(layer, seg_of_tile(i), which_gate), 0, j)),
        ],
        out_specs=pl.BlockSpec((tm, tn), lambda i, j: (i, j)),
        out_shape=jax.ShapeDtypeStruct((rows, D_MODEL), F32),
        compiler_params=_params("arbitrary", "arbitrary"),
        name="matmul_residual",
    )(a, w, res, mods)


def _ffn_up_kernel(h_ref, wu_ref, wv_ref, cw_ref, o_ref, *, seq):
    cu = _conv3(_dot_rows(h_ref, wu_ref[...].astype(BF16)), cw_ref, seq)
    o_ref[...] = (cu * jax.nn.sigmoid(cu) * _dot_rows(h_ref, wv_ref[...].astype(BF16))).astype(BF16)


def _ffn_up(h, w_up, conv, layer, seq, tm, tn):
    rows = h.shape[0]
    nb = D_FF // tn
    return pl.pallas_call(
        functools.partial(_ffn_up_kernel, seq=seq),
        grid=(rows // tm, nb),
        in_specs=[
            pl.BlockSpec((tm, D_MODEL), lambda b, f: (b, 0)),
            pl.BlockSpec((None, D_MODEL, tn), lambda b, f: (layer, 0, f)),
            pl.BlockSpec((None, D_MODEL, tn), lambda b, f: (layer, 0, nb + f)),
            pl.BlockSpec((None, 3, tn), lambda b, f: (layer, 0, f)),
        ],
        out_specs=pl.BlockSpec((tm, tn), lambda b, f: (b, f)),
        out_shape=jax.ShapeDtypeStruct((rows, D_FF), BF16),
        compiler_params=_params("arbitrary", "arbitrary"),
        name="ffn_up",
    )(h, w_up, w_up, conv)


_ROPE_SWAP = np.concatenate([np.arange(16, 32), np.arange(0, 16), np.arange(48, 64), np.arange(32, 48)])


def _rope_cos_sin(length):
    rows = length // GRID_W
    row = jnp.repeat(jnp.arange(rows, dtype=jnp.int32), GRID_W).astype(F32)
    col = jnp.tile(jnp.arange(GRID_W, dtype=jnp.int32), rows).astype(F32)
    inv = ROPE_THETA ** (-jnp.arange(0, ROPE_AXIS, 2, dtype=F32) / ROPE_AXIS)
    ang_r = row[:, None] * inv[None]
    ang_c = col[:, None] * inv[None]
    cr, sr, cc, sc = jnp.cos(ang_r), jnp.sin(ang_r), jnp.cos(ang_c), jnp.sin(ang_c)
    return jnp.concatenate([cr, cr, cc, cc], axis=-1), jnp.concatenate([-sr, sr, -sc, sc], axis=-1)


def _identity_cos_sin(length):
    return jnp.ones((length, QK_ROPE), F32), jnp.zeros((length, QK_ROPE), F32)


def _rope_tables(cos, sin):
    return jnp.concatenate([cos, cos, sin, sin], axis=-1), jnp.concatenate([cos, sin], axis=-1)


def _prepared_weights(w_in, w_uq, w_ukv, q_head_g, k_head_g):
    k_rope = w_in[:, :, OFF_KROPE:OFF_G]
    uq = jnp.transpose(w_uq, (0, 2, 1, 3)).reshape(DEPTH, N_HEADS // 2, 2, Q_LORA, QK_DIM)
    even, odd = uq[:, :, 0], uq[:, :, 1]
    swapped = lambda w: w[..., QK_NOPE:][..., _ROPE_SWAP]
    q_rope, k_rope_g = q_head_g[:, QK_NOPE:], k_head_g[:, QK_NOPE:]
    q_swap, k_swap = q_rope[:, _ROPE_SWAP], k_rope_g[:, _ROPE_SWAP]
    return dict(
        wa=w_in[:, :, OFF_A:OFF_B].astype(BF16),
        wg=w_in[:, :, OFF_G:].astype(BF16),
        wkr=jnp.concatenate([k_rope, k_rope[:, :, _ROPE_SWAP]], axis=-1).astype(BF16),
        wuq=jnp.concatenate([even[..., :QK_NOPE], odd[..., :QK_NOPE], even[..., QK_NOPE:], odd[..., QK_NOPE:],
                             swapped(even), swapped(odd)], axis=-1).astype(BF16),
        wukv=w_ukv.reshape(DEPTH, KV_LORA, N_HEADS * (QK_NOPE + V_DIM)).astype(BF16),
        q_gn=q_head_g[:, None, :QK_NOPE], k_gn=k_head_g[:, None, :QK_NOPE],
        q_g2=jnp.concatenate([q_rope, q_rope, q_swap, q_swap], axis=-1)[:, None],
        k_g2=jnp.concatenate([k_rope_g, k_swap], axis=-1)[:, None],
    )


def _tiles(batch, seq):
    rows = batch * seq
    return dict(
        tm=min(rows, 1024),
        seq_tm=seq if seq >= 1024 else rows,
        proj_tm=min(seq, 512),
        tq=min(seq, 512),
    )


def kernel(x, c, ctx, c_ctx, norm1_g, norm2_g, w_mod, b_mod, w_in, pool_w, pool_scale, conv_w, q_lora_g, w_uq,
           kv_lora_g, w_ukv, q_head_g, k_head_g, w_branch_a, w_branch_b, w_branch_c, w_out, w_ffn_up, ffn_conv,
           w_ffn_down):
    batch, seq, _ = x.shape
    ctx_len = ctx.shape[1]
    assert batch + 1 <= MOD_ROWS
    cv = jnp.concatenate([c, c_ctx[None], jnp.zeros((MOD_ROWS - batch - 1, D_MODEL), F32)], axis=0)
    mods = _modulation(cv, w_mod, b_mod)
    pw = _prepared_weights(w_in, w_uq, w_ukv, q_head_g, k_head_g)

    def kv_proj(h, layer, cs, n_tok, tiles):
        return _kv_proj(h, w_in, pw['wkr'], kv_lora_g, pw['wukv'], pw['k_gn'], pw['k_g2'], cs[1], layer, batch,
                        n_tok, tiles['proj_tm'])

    def stream_layer(xs, layer, seg_of_tile, cs, n_tok, ctx_kv, tiles):
        tm, seq_tm = tiles['tm'], tiles['seq_tm']
        h = _norm_mod(xs, norm1_g, mods, seg_of_tile, layer, 0, tm)
        pa = _branch_a(h, pw['wa'], pool_w, pool_scale, layer, n_tok)
        pb = _branch_b(h, w_in, conv_w, layer, n_tok, seq_tm, 256)
        q = _q_proj(h, w_in, q_lora_g, pw['wuq'], pw['q_gn'], pw['q_g2'], cs[0], layer, batch, n_tok,
                    tiles['proj_tm'])
        kv = kv_proj(h, layer, cs, n_tok, tiles)
        kvs = [kv] if ctx_kv is None else [kv, ctx_kv]
        oc = _attention(q, kvs, tiles['tq']).reshape(batch * n_tok, N_HEADS * V_DIM)
        merged = _merge(h, pa, pb, oc, pw['wg'], w_branch_a, w_branch_b, w_branch_c, layer, tm, 256)
        xs = _matmul_residual(merged, w_out, xs, mods, seg_of_tile, layer, 2, tm, 512)
        h2 = _norm_mod(xs, norm2_g, mods, seg_of_tile, layer, 3, tm)
        hidden = _ffn_up(h2, w_ffn_up, ffn_conv, layer, n_tok, seq_tm, 256)
        xs = _matmul_residual(hidden, w_ffn_down, xs, mods, seg_of_tile, layer, 5, tm, 256)
        return xs, kv

    cs_x = _rope_tables(*_rope_cos_sin(seq))
    cs_c = _rope_tables(*_identity_cos_sin(ctx_len))
    x_tiles = _tiles(batch, seq)
    c_tiles = _tiles(batch, ctx_len)
    x_seg = lambda i: i // (seq // x_tiles['tm'])
    c_seg = lambda i: batch

    xs = x.reshape(batch * seq, D_MODEL)
    cx = ctx.reshape(batch * ctx_len, D_MODEL)
    for layer in range(DEPTH):
        if layer == DEPTH - 1:
            hc = _norm_mod(cx, norm1_g, mods, c_seg, layer, 0, c_tiles['tm'])
            ctx_kv = kv_proj(hc, layer, cs_c, ctx_len, c_tiles)
        else:
            cx, ctx_kv = stream_layer(cx, layer, c_seg, cs_c, ctx_len, None, c_tiles)
        xs, _ = stream_layer(xs, layer, x_seg, cs_x, seq, tuple(ctx_kv), x_tiles)
    return xs.reshape(batch, seq, D_MODEL)
```

```python
import functools
import math

import jax
import jax.numpy as jnp
import numpy as np
from jax import lax
from jax.experimental import pallas as pl
from jax.experimental.pallas import tpu as pltpu

D_MODEL = 2048
DEPTH = 2
GRID_W = 64
POOL_WINDOWS = (2, 4, 8, 16)
POOL_GROUP = D_MODEL // 8
POOL_WIDTH = POOL_GROUP * len(POOL_WINDOWS)
CONV_WIDTH = D_MODEL // 2
N_HEADS = 16
QK_NOPE = 128
QK_ROPE = 64
ROPE_AXIS = QK_ROPE // 2
V_DIM = 128
QK_DIM = QK_NOPE + QK_ROPE
Q_LORA = 512
KV_LORA = 512
ROPE_THETA = 10000.0
ATTN_SCALE = QK_DIM ** -0.5
D_FF = 5632
N_BRANCH = 3
EPS = 1e-6
OFF_A = 0
OFF_B = OFF_A + POOL_WIDTH
OFF_Q = OFF_B + 3 * CONV_WIDTH
OFF_KV = OFF_Q + Q_LORA
OFF_KROPE = OFF_KV + KV_LORA
OFF_G = OFF_KROPE + QK_ROPE

LANES = 128
QK_STORE = 2 * LANES
MXU_ROWS = 512
POOL_PAD = 16
MOD_ROWS = 8
N_MOD = 6
VMEM_LIMIT = 56 * 1024 * 1024
KEY_CHUNK = 512

BF16 = jnp.bfloat16
F32 = jnp.float32


def _dot(a, b):
    return jnp.dot(a, b, preferred_element_type=F32)


def _dot_nt(a, b_t):
    return lax.dot_general(a, b_t, (((1,), (1,)), ((), ())), preferred_element_type=F32)


def _dot_rows(a_ref, w, dot=_dot):
    rows = a_ref.shape[0]
    if rows <= MXU_ROWS:
        return dot(a_ref[...], w)
    return jnp.concatenate([dot(a_ref[r:r + MXU_ROWS, :], w) for r in range(0, rows, MXU_ROWS)], axis=0)


def _in_proj(h_ref, wt_ref):
    return _dot_rows(h_ref, wt_ref[...].astype(BF16), _dot_nt)


def _params(*sem):
    return pltpu.CompilerParams(dimension_semantics=sem, vmem_limit_bytes=VMEM_LIMIT)


def _mod_row(layer, seg, which):
    return (layer * MOD_ROWS + seg) * N_MOD + which


def _mod_kernel(cv_ref, w_ref, b_ref, o_ref):
    a = cv_ref[...]
    a = a * jax.nn.sigmoid(a)
    o_ref[...] = _dot(a.astype(BF16), w_ref[...].astype(BF16)) + b_ref[...]


def _modulation(cv, w_mod, b_mod):
    tn = 1024
    n_out = N_MOD * D_MODEL
    out = pl.pallas_call(
        _mod_kernel,
        grid=(DEPTH, n_out // tn),
        in_specs=[
            pl.BlockSpec((MOD_ROWS, D_MODEL), lambda l, j: (0, 0)),
            pl.BlockSpec((None, D_MODEL, tn), lambda l, j: (l, 0, j)),
            pl.BlockSpec((None, 1, tn), lambda l, j: (l, 0, j)),
        ],
        out_specs=pl.BlockSpec((None, MOD_ROWS, tn), lambda l, j: (l, 0, j)),
        out_shape=jax.ShapeDtypeStruct((DEPTH, MOD_ROWS, n_out), F32),
        compiler_params=_params("arbitrary", "arbitrary"),
        name="modulation",
    )(cv, w_mod, b_mod.reshape(DEPTH, 1, n_out))
    return out.reshape(DEPTH * MOD_ROWS * N_MOD, 1, D_MODEL)


def _norm_mod_kernel(x_ref, g_ref, sh_ref, sc_ref, o_ref):
    x = x_ref[...]
    y = x * lax.rsqrt(jnp.mean(x * x, axis=-1, keepdims=True) + EPS)
    y = y * g_ref[0]
    o_ref[...] = (y * (1 + sc_ref[0]) + sh_ref[0]).astype(BF16)


def _norm_mod(x, g, mods, seg_of_tile, layer, which_shift, tm):
    rows = x.shape[0]
    return pl.pallas_call(
        _norm_mod_kernel,
        grid=(rows // tm,),
        in_specs=[
            pl.BlockSpec((tm, D_MODEL), lambda i: (i, 0)),
            pl.BlockSpec((1, 1, D_MODEL), lambda i: (layer, 0, 0)),
            pl.BlockSpec((1, 1, D_MODEL), lambda i: (_mod_row(layer, seg_of_tile(i), which_shift), 0, 0)),
            pl.BlockSpec((1, 1, D_MODEL), lambda i: (_mod_row(layer, seg_of_tile(i), which_shift + 1), 0, 0)),
        ],
        out_specs=pl.BlockSpec((tm, D_MODEL), lambda i: (i, 0)),
        out_shape=jax.ShapeDtypeStruct((rows, D_MODEL), BF16),
        compiler_params=_params("arbitrary"),
        name="norm_mod",
    )(x, g.reshape(DEPTH, 1, D_MODEL), mods, mods)


def _branch_a_kernel(h_ref, *refs, seq):
    n_win = len(POOL_WINDOWS)
    wa_refs = refs[:n_win]
    pw_ref, ps_ref, o_ref, ext_ref = refs[n_win:]
    n_ext = seq + 2 * POOL_PAD
    zeros = jnp.zeros((POOL_PAD, POOL_GROUP), F32)
    ext_ref[0:POOL_PAD, :] = zeros
    ext_ref[seq + POOL_PAD:n_ext, :] = zeros
    t = lax.broadcasted_iota(jnp.int32, (seq, 1), 0)
    for g, w in enumerate(POOL_WINDOWS):
        cols = slice(g * POOL_GROUP, (g + 1) * POOL_GROUP)
        a = _in_proj(h_ref, wa_refs[g])
        ext_ref[POOL_PAD:seq + POOL_PAD, :] = a
        e = ext_ref[...]
        c = e + pltpu.roll(e, 1, 0)
        half = 1
        while 2 * half < w:
            c = pltpu.roll(c, half, 0) + pltpu.roll(c, n_ext - half, 0)
            half *= 2
        cnt = (jnp.minimum(t + w // 2, seq) - jnp.maximum(t - w // 2, 0)).astype(F32)
        pooled = c[POOL_PAD:seq + POOL_PAD, :] / cnt - a
        mixed = _dot(pooled.astype(BF16), pw_ref[g].astype(BF16)) * ps_ref[0, :, cols]
        o_ref[:, cols] = mixed.astype(BF16)


def _branch_a(h, w_in_t, pool_w, pool_scale, layer, seq):
    rows = h.shape[0]
    n_win = len(POOL_WINDOWS)
    group_rows = lambda g: pl.BlockSpec((None, POOL_GROUP, D_MODEL), lambda b: (layer, OFF_A // POOL_GROUP + g, 0))
    return pl.pallas_call(
        functools.partial(_branch_a_kernel, seq=seq),
        grid=(rows // seq,),
        in_specs=[pl.BlockSpec((seq, D_MODEL), lambda b: (b, 0))] + [group_rows(g) for g in range(n_win)] + [
            pl.BlockSpec((None, n_win, POOL_GROUP, POOL_GROUP), lambda b: (layer, 0, 0, 0)),
            pl.BlockSpec((1, 1, POOL_WIDTH), lambda b: (layer, 0, 0)),
        ],
        out_specs=pl.BlockSpec((seq, POOL_WIDTH), lambda b: (b, 0)),
        out_shape=jax.ShapeDtypeStruct((rows, POOL_WIDTH), BF16),
        scratch_shapes=[pltpu.VMEM((seq + 2 * POOL_PAD, POOL_GROUP), F32)],
        compiler_params=_params("arbitrary"),
        name="branch_a",
    )(h, *([w_in_t] * n_win), pool_w, pool_scale.reshape(DEPTH, 1, POOL_WIDTH))


def _conv3(u, cw_ref, seq):
    rows = u.shape[0]
    t = lax.broadcasted_iota(jnp.int32, (rows, 1), 0) % seq
    prev = jnp.where(t == 0, 0.0, pltpu.roll(u, 1, 0))
    nxt = jnp.where(t == seq - 1, 0.0, pltpu.roll(u, rows - 1, 0))
    return prev * cw_ref[0:1, :] + u * cw_ref[1:2, :] + nxt * cw_ref[2:3, :]


def _branch_b_kernel(h_ref, wgb_ref, wgc_ref, wx_ref, cw_ref, o_ref, *, seq):
    u = _in_proj(h_ref, wgc_ref) * _in_proj(h_ref, wx_ref)
    o_ref[...] = (_in_proj(h_ref, wgb_ref) * _conv3(u, cw_ref, seq)).astype(BF16)


def _branch_b(h, w_in_t, conv_w, layer, seq, tm, tn):
    rows = h.shape[0]
    nb = CONV_WIDTH // tn
    first = OFF_B // tn
    w_spec = lambda k: pl.BlockSpec((None, tn, D_MODEL), lambda b, c: (layer, first + k * nb + c, 0))
    return pl.pallas_call(
        functools.partial(_branch_b_kernel, seq=seq),
        grid=(rows // tm, nb),
        in_specs=[
            pl.BlockSpec((tm, D_MODEL), lambda b, c: (b, 0)),
            w_spec(0), w_spec(1), w_spec(2),
            pl.BlockSpec((None, 3, tn), lambda b, c: (layer, 0, c)),
        ],
        out_specs=pl.BlockSpec((tm, tn), lambda b, c: (b, c)),
        out_shape=jax.ShapeDtypeStruct((rows, CONV_WIDTH), BF16),
        compiler_params=_params("arbitrary", "arbitrary"),
        name="branch_b",
    )(h, w_in_t, w_in_t, w_in_t, conv_w)


def _lo_lanes():
    return lax.broadcasted_iota(jnp.int32, (1, LANES), 1) < QK_ROPE


def _q_proj_kernel(h_ref, wq_ref, lg_ref, wuq_ref, gn_ref, g2_ref, cs_ref, o_ref):
    zq = _in_proj(h_ref, wq_ref)
    cq = zq * lax.rsqrt(jnp.mean(zq * zq, axis=-1, keepdims=True) + EPS) * lg_ref[0]
    cq = cq.astype(BF16)
    gain_table = g2_ref[0] * cs_ref[...]
    lo = _lo_lanes()
    for pair in range(N_HEADS // 2):
        y = _dot(cq, wuq_ref[pair])
        nope = (y[:, 0:LANES], y[:, LANES:2 * LANES])
        yr, ys = y[:, 2 * LANES:3 * LANES], y[:, 3 * LANES:]
        sq_r = yr * yr
        rot = yr * gain_table[:, :LANES] + ys * gain_table[:, LANES:]
        for par in range(2):
            sq = nope[par] * nope[par] + jnp.where(lo if par == 0 else ~lo, sq_r, 0.0)
            r = lax.rsqrt(jnp.sum(sq, axis=-1, keepdims=True) / QK_DIM + EPS)
            hd = 2 * pair + par
            o_ref[0, hd, :, 0:LANES] = (nope[par] * r * gn_ref[0]).astype(BF16)
            o_ref[0, hd, :, LANES:QK_STORE] = jnp.where(lo if par == 0 else ~lo, rot * r, 0.0).astype(BF16)


def _q_proj(h, w_in_t, lora_g, wuq, gn, g2, cs, layer, batch, seq, tm):
    nt = seq // tm
    vec = lambda width: pl.BlockSpec((1, 1, width), lambda i: (layer, 0, 0))
    return pl.pallas_call(
        _q_proj_kernel,
        grid=(batch * nt,),
        in_specs=[
            pl.BlockSpec((tm, D_MODEL), lambda i: (i, 0)),
            pl.BlockSpec((None, Q_LORA, D_MODEL), lambda i: (layer, OFF_Q // Q_LORA, 0)),
            vec(Q_LORA),
            pl.BlockSpec((None, N_HEADS // 2, Q_LORA, 4 * LANES), lambda i: (layer, 0, 0, 0)),
            vec(LANES), vec(2 * LANES),
            pl.BlockSpec((tm, 2 * LANES), lambda i: (i % nt, 0)),
        ],
        out_specs=pl.BlockSpec((1, N_HEADS, tm, QK_STORE), lambda i: (i // nt, 0, i % nt, 0)),
        out_shape=jax.ShapeDtypeStruct((batch, N_HEADS, seq, QK_STORE), BF16),
        compiler_params=_params("arbitrary"),
        name="q_proj",
    )(h, w_in_t, lora_g.reshape(DEPTH, 1, Q_LORA), wuq, gn, g2, cs)


def _kv_proj_kernel(h_ref, wkv_ref, wkr_ref, lg_ref, wukv_ref, gn_ref, g2_ref, cs_ref, k_ref, v_ref):
    zc = _in_proj(h_ref, wkv_ref)
    q4 = QK_ROPE // 4
    wkr = jnp.concatenate([wkr_ref[...]] + [wkr_ref[s * q4:(s + 1) * q4, :] for s in (1, 0, 3, 2)], axis=0)
    kr = _dot_nt(h_ref[...], wkr.astype(BF16))
    ckv = zc * lax.rsqrt(jnp.mean(zc * zc, axis=-1, keepdims=True) + EPS) * lg_ref[0]
    ckv = ckv.astype(BF16)
    lo = _lo_lanes()
    ss_rope = jnp.sum(jnp.where(lo, kr * kr, 0.0), axis=-1, keepdims=True)
    u = kr * (g2_ref[0] * cs_ref[...])
    rot = u + pltpu.roll(u, LANES // 2, 1)
    rope = (jnp.where(lo, rot, 0.0), jnp.where(lo, 0.0, rot))
    ones = jnp.ones((h_ref.shape[0], LANES), BF16)
    for hd in range(N_HEADS):
        kv = _dot(ckv, wukv_ref[:, hd * 2 * LANES:(hd + 1) * 2 * LANES])
        kn, v = kv[:, :QK_NOPE], kv[:, QK_NOPE:]
        ss = jnp.sum(kn * kn, axis=-1, keepdims=True) + ss_rope
        r = lax.rsqrt(ss / QK_DIM + EPS)
        k_ref[0, hd, :, 0:LANES] = (kn * r * gn_ref[0]).astype(BF16)
        k_ref[0, hd, :, LANES:QK_STORE] = (rope[hd % 2] * r).astype(BF16)
        v_ref[0, hd, :, 0:V_DIM] = v.astype(BF16)
        v_ref[0, hd, :, V_DIM:2 * V_DIM] = ones


def _kv_proj(h, w_in_t, lora_g, wukv, gn, g2, cs, layer, batch, seq, tm):
    nt = seq // tm
    vec = lambda width: pl.BlockSpec((1, 1, width), lambda i: (layer, 0, 0))
    return pl.pallas_call(
        _kv_proj_kernel,
        grid=(batch * nt,),
        in_specs=[
            pl.BlockSpec((tm, D_MODEL), lambda i: (i, 0)),
            pl.BlockSpec((None, KV_LORA, D_MODEL), lambda i: (layer, OFF_KV // KV_LORA, 0)),
            pl.BlockSpec((None, QK_ROPE, D_MODEL), lambda i: (layer, OFF_KROPE // QK_ROPE, 0)),
            vec(KV_LORA),
            pl.BlockSpec((None, KV_LORA, N_HEADS * 2 * LANES), lambda i: (layer, 0, 0)),
            vec(LANES), vec(LANES),
            pl.BlockSpec((tm, LANES), lambda i: (i % nt, 0)),
        ],
        out_specs=[
            pl.BlockSpec((1, N_HEADS, tm, QK_STORE), lambda i: (i // nt, 0, i % nt, 0)),
            pl.BlockSpec((1, N_HEADS, tm, 2 * V_DIM), lambda i: (i // nt, 0, i % nt, 0)),
        ],
        out_shape=[
            jax.ShapeDtypeStruct((batch, N_HEADS, seq, QK_STORE), BF16),
            jax.ShapeDtypeStruct((batch, N_HEADS, seq, 2 * V_DIM), BF16),
        ],
        compiler_params=_params("arbitrary"),
        name="kv_proj",
    )(h, w_in_t, w_in_t, lora_g.reshape(DEPTH, 1, KV_LORA), wukv, gn, g2, cs)


def _attn_kernel(q_ref, *refs, tq):
    o_ref = refs[-1]
    kv_refs = refs[:-1]
    scale = ATTN_SCALE * math.log2(math.e)
    for r in range(0, q_ref.shape[2], tq):
        q = q_ref[0, 0, r:r + tq, :]
        m = acc = None
        for s in range(0, len(kv_refs), 2):
            k_ref, v_ref = kv_refs[s], kv_refs[s + 1]
            n_keys = k_ref.shape[2]
            tk = min(KEY_CHUNK, n_keys)
            for j in range(n_keys // tk):
                k = k_ref[0, 0, j * tk:(j + 1) * tk, :]
                v = v_ref[0, 0, j * tk:(j + 1) * tk, :]
                sc = _dot_nt(q, k)
                m_chunk = jnp.max(sc, axis=-1, keepdims=True)
                if m is None:
                    m = m_chunk
                    acc = _dot(jnp.exp2((sc - m) * scale).astype(BF16), v)
                else:
                    m_new = jnp.maximum(m, m_chunk)
                    alpha = jnp.exp2((m - m_new) * scale)
                    acc = alpha * acc + _dot(jnp.exp2((sc - m_new) * scale).astype(BF16), v)
                    m = m_new
        o_ref[0, r:r + tq, :] = (acc[:, :V_DIM] / acc[:, V_DIM:]).astype(BF16)


def _attention(q, kvs, tq):
    batch, _, seq, _ = q.shape
    head_spec = lambda rows, width: pl.BlockSpec((1, 1, rows, width), lambda b, hd: (b, hd, 0, 0))
    in_specs = [head_spec(seq, QK_STORE)]
    args = [q]
    for k, v in kvs:
        in_specs += [head_spec(k.shape[2], QK_STORE), head_spec(v.shape[2], 2 * V_DIM)]
        args += [k, v]
    return pl.pallas_call(
        functools.partial(_attn_kernel, tq=tq),
        grid=(batch, N_HEADS),
        in_specs=in_specs,
        out_specs=pl.BlockSpec((1, seq, V_DIM), lambda b, hd: (b, 0, hd)),
        out_shape=jax.ShapeDtypeStruct((batch, seq, N_HEADS * V_DIM), BF16),
        compiler_params=_params("arbitrary", "arbitrary"),
        name="attention",
    )(*args)


def _merge_kernel(h_ref, pa_ref, pb_ref, oc_ref, wg0_ref, wg1_ref, wg2_ref, wa_ref, wb_ref, wc_ref, o_ref):
    gated = lambda wg_ref, y_ref, w_ref: (jax.nn.sigmoid(_dot_rows(h_ref, wg_ref[...], _dot_nt))
                                          * _dot_rows(y_ref, w_ref[...].astype(BF16)))
    merged = gated(wg0_ref, pa_ref, wa_ref)
    merged = merged + gated(wg1_ref, pb_ref, wb_ref)
    merged = merged + gated(wg2_ref, oc_ref, wc_ref)
    o_ref[...] = merged.astype(BF16)


def _merge(h, pa, pb, oc, wg_t, wba, wbb, wbc, layer, tm, tn):
    rows = h.shape[0]
    nb = D_MODEL // tn
    row_spec = lambda width: pl.BlockSpec((tm, width), lambda i, j: (i, 0))
    gate_spec = lambda k: pl.BlockSpec((None, tn, D_MODEL), lambda i, j: (layer, k * nb + j, 0))
    col_spec = lambda depth: pl.BlockSpec((None, depth, tn), lambda i, j: (layer, 0, j))
    return pl.pallas_call(
        _merge_kernel,
        grid=(rows // tm, nb),
        in_specs=[
            row_spec(D_MODEL), row_spec(POOL_WIDTH), row_spec(CONV_WIDTH), row_spec(N_HEADS * V_DIM),
            gate_spec(0), gate_spec(1), gate_spec(2),
            col_spec(POOL_WIDTH), col_spec(CONV_WIDTH), col_spec(N_HEADS * V_DIM),
        ],
        out_specs=pl.BlockSpec((tm, tn), lambda i, j: (i, j)),
        out_shape=jax.ShapeDtypeStruct((rows, D_MODEL), BF16),
        compiler_params=_params("arbitrary", "arbitrary"),
        name="merge",
    )(h, pa, pb, oc, wg_t, wg_t, wg_t, wba, wbb, wbc)


def _matmul_residual_kernel(a_ref, w_ref, res_ref, gate_ref, o_ref):
    o_ref[...] = res_ref[...] + gate_ref[0] * _dot_rows(a_ref, w_ref[...].astype(BF16))


def _matmul_residual(a, w, res, mods, seg_of_tile, layer, which_gate, tm, tn):
    rows, depth = a.shape
    return pl.pallas_call(
        _matmul_residual_kernel,
        grid=(rows // tm, D_MODEL // tn),
        in_specs=[
            pl.BlockSpec((tm, depth), lambda i, j: (i, 0)),
            pl.BlockSpec((None, depth, tn), lambda i, j: (layer, 0, j)),
            pl.BlockSpec((tm, tn), lambda i, j: (i, j)),
            pl.BlockSpec((1, 1, tn), lambda i, j: (_mod_row(layer, seg_of_tile(i), which_gate), 0, j)),
        ],
        out_specs=pl.BlockSpec((tm, tn), lambda i, j: (i, j)),
        out_shape=jax.ShapeDtypeStruct((rows, D_MODEL), F32),
        compiler_params=_params("arbitrary", "arbitrary"),
        name="matmul_residual",
    )(a, w, res, mods)


def _ffn_up_kernel(h_ref, wu_ref, wv_ref, cw_ref, o_ref, *, seq):
    cu = _conv3(_dot_rows(h_ref, wu_ref[...].astype(BF16)), cw_ref, seq)
    o_ref[...] = (cu * jax.nn.sigmoid(cu) * _dot_rows(h_ref, wv_ref[...].astype(BF16))).astype(BF16)


def _ffn_up(h, w_up, conv, layer, seq, tm, tn):
    rows = h.shape[0]
    nb = D_FF // tn
    return pl.pallas_call(
        functools.partial(_ffn_up_kernel, seq=seq),
        grid=(rows // tm, nb),
        in_specs=[
            pl.BlockSpec((tm, D_MODEL), lambda b, f: (b, 0)),
            pl.BlockSpec((None, D_MODEL, tn), lambda b, f: (layer, 0, f)),
            pl.BlockSpec((None, D_MODEL, tn), lambda b, f: (layer, 0, nb + f)),
            pl.BlockSpec((None, 3, tn), lambda b, f: (layer, 0, f)),
        ],
        out_specs=pl.BlockSpec((tm, tn), lambda b, f: (b, f)),
        out_shape=jax.ShapeDtypeStruct((rows, D_FF), BF16),
        compiler_params=_params("arbitrary", "arbitrary"),
        name="ffn_up",
    )(h, w_up, w_up, conv)


_ROPE_SWAP = np.concatenate([np.arange(16, 32), np.arange(0, 16), np.arange(48, 64), np.arange(32, 48)])


def _rope_cos_sin(length):
    rows = length // GRID_W
    row = jnp.repeat(jnp.arange(rows, dtype=jnp.int32), GRID_W).astype(F32)
    col = jnp.tile(jnp.arange(GRID_W, dtype=jnp.int32), rows).astype(F32)
    inv = ROPE_THETA ** (-jnp.arange(0, ROPE_AXIS, 2, dtype=F32) / ROPE_AXIS)
    ang_r = row[:, None] * inv[None]
    ang_c = col[:, None] * inv[None]
    cr, sr, cc, sc = jnp.cos(ang_r), jnp.sin(ang_r), jnp.cos(ang_c), jnp.sin(ang_c)
    return jnp.concatenate([cr, cr, cc, cc], axis=-1), jnp.concatenate([-sr, sr, -sc, sc], axis=-1)


def _identity_cos_sin(length):
    return jnp.ones((length, QK_ROPE), F32), jnp.zeros((length, QK_ROPE), F32)


def _rope_tables(cos, sin):
    return jnp.concatenate([cos, cos, sin, sin], axis=-1), jnp.concatenate([cos, sin], axis=-1)


def _prepared_weights(w_in_t, w_uq, w_ukv, w_ffn_down, q_head_g, k_head_g):
    uq = jnp.transpose(w_uq, (0, 2, 1, 3)).reshape(DEPTH, N_HEADS // 2, 2, Q_LORA, QK_DIM)
    even, odd = uq[:, :, 0], uq[:, :, 1]
    swapped = lambda w: w[..., QK_NOPE:][..., _ROPE_SWAP]
    q_rope, k_rope_g = q_head_g[:, QK_NOPE:], k_head_g[:, QK_NOPE:]
    q_swap, k_swap = q_rope[:, _ROPE_SWAP], k_rope_g[:, _ROPE_SWAP]
    return dict(
        wg_t=w_in_t[:, OFF_G:, :].astype(BF16),
        w_down=w_ffn_down.astype(BF16),
        wuq=jnp.concatenate([even[..., :QK_NOPE], odd[..., :QK_NOPE], even[..., QK_NOPE:], odd[..., QK_NOPE:],
                             swapped(even), swapped(odd)], axis=-1).astype(BF16),
        wukv=w_ukv.reshape(DEPTH, KV_LORA, N_HEADS * (QK_NOPE + V_DIM)).astype(BF16),
        q_gn=q_head_g[:, None, :QK_NOPE], k_gn=k_head_g[:, None, :QK_NOPE],
        q_g2=jnp.concatenate([q_rope, q_rope, q_swap, q_swap], axis=-1)[:, None],
        k_g2=jnp.concatenate([k_rope_g, k_swap], axis=-1)[:, None],
    )


def _tiles(batch, seq):
    rows = batch * seq
    return dict(
        tm=min(rows, 1024),
        seq_tm=seq if seq >= 1024 else rows,
        proj_tm=min(seq, 512),
        tq=min(seq, 512),
    )


def kernel(x, c, ctx, c_ctx, norm1_g, norm2_g, w_mod, b_mod, w_in, pool_w, pool_scale, conv_w, q_lora_g, w_uq,
           kv_lora_g, w_ukv, q_head_g, k_head_g, w_branch_a, w_branch_b, w_branch_c, w_out, w_ffn_up, ffn_conv,
           w_ffn_down):
    batch, seq, _ = x.shape
    ctx_len = ctx.shape[1]
    assert batch + 1 <= MOD_ROWS
    cv = jnp.concatenate([c, c_ctx[None], jnp.zeros((MOD_ROWS - batch - 1, D_MODEL), F32)], axis=0)
    mods = _modulation(cv, w_mod, b_mod)
    w_in_t = jnp.swapaxes(w_in, 1, 2)
    pw = _prepared_weights(w_in_t, w_uq, w_ukv, w_ffn_down, q_head_g, k_head_g)

    def kv_proj(h, layer, cs, n_tok, tiles):
        return _kv_proj(h, w_in_t, kv_lora_g, pw['wukv'], pw['k_gn'], pw['k_g2'], cs[1], layer, batch, n_tok,
                        tiles['proj_tm'])

    def stream_layer(xs, layer, seg_of_tile, cs, n_tok, ctx_kv, tiles):
        tm, seq_tm = tiles['tm'], tiles['seq_tm']
        h = _norm_mod(xs, norm1_g, mods, seg_of_tile, layer, 0, tm)
        pa = _branch_a(h, w_in_t, pool_w, pool_scale, layer, n_tok)
        pb = _branch_b(h, w_in_t, conv_w, layer, n_tok, seq_tm, 256)
        q = _q_proj(h, w_in_t, q_lora_g, pw['wuq'], pw['q_gn'], pw['q_g2'], cs[0], layer, batch, n_tok,
                    tiles['proj_tm'])
        kv = kv_proj(h, layer, cs, n_tok, tiles)
        kvs = [kv] if ctx_kv is None else [kv, ctx_kv]
        oc = _attention(q, kvs, tiles['tq']).reshape(batch * n_tok, N_HEADS * V_DIM)
        merged = _merge(h, pa, pb, oc, pw['wg_t'], w_branch_a, w_branch_b, w_branch_c, layer, tm, 256)
        xs = _matmul_residual(merged, w_out, xs, mods, seg_of_tile, layer, 2, tm, 512)
        h2 = _norm_mod(xs, norm2_g, mods, seg_of_tile, layer, 3, tm)
        hidden = _ffn_up(h2, w_ffn_up, ffn_conv, layer, n_tok, seq_tm, 256)
        xs = _matmul_residual(hidden, pw['w_down'], xs, mods, seg_of_tile, layer, 5, tm, 256)
        return xs, kv

    cs_x = _rope_tables(*_rope_cos_sin(seq))
    cs_c = _rope_tables(*_identity_cos_sin(ctx_len))
    x_tiles = _tiles(batch, seq)
    c_tiles = _tiles(batch, ctx_len)
    x_seg = lambda i: i // (seq // x_tiles['tm'])
    c_seg = lambda i: batch

    xs = x.reshape(batch * seq, D_MODEL)
    cx = ctx.reshape(batch * ctx_len, D_MODEL)
    for layer in range(DEPTH):
        if layer == DEPTH - 1:
            hc = _norm_mod(cx, norm1_g, mods, c_seg, layer, 0, c_tiles['tm'])
            ctx_kv = kv_proj(hc, layer, cs_c, ctx_len, c_tiles)
        else:
            cx, ctx_kv = stream_layer(cx, layer, c_seg, cs_c, ctx_len, None, c_tiles)
        xs, _ = stream_layer(xs, layer, x_seg, cs_x, seq, tuple(ctx_kv), x_tiles)
    return xs.reshape(batch, seq, D_MODEL)
```

```python
import functools
import math

import jax
import jax.numpy as jnp
import numpy as np
from jax import lax
from jax.experimental import pallas as pl
from jax.experimental.pallas import tpu as pltpu

D_MODEL = 2048
DEPTH = 2
GRID_W = 64
POOL_WINDOWS = (2, 4, 8, 16)
POOL_GROUP = D_MODEL // 8
POOL_WIDTH = POOL_GROUP * len(POOL_WINDOWS)
CONV_WIDTH = D_MODEL // 2
N_HEADS = 16
QK_NOPE = 128
QK_ROPE = 64
ROPE_AXIS = QK_ROPE // 2
V_DIM = 128
QK_DIM = QK_NOPE + QK_ROPE
Q_LORA = 512
KV_LORA = 512
ROPE_THETA = 10000.0
ATTN_SCALE = QK_DIM ** -0.5
D_FF = 5632
N_BRANCH = 3
EPS = 1e-6
OFF_A = 0
OFF_B = OFF_A + POOL_WIDTH
OFF_Q = OFF_B + 3 * CONV_WIDTH
OFF_KV = OFF_Q + Q_LORA
OFF_KROPE = OFF_KV + KV_LORA
OFF_G = OFF_KROPE + QK_ROPE

LANES = 128
QK_STORE = 2 * LANES
MXU_ROWS = 512
POOL_PAD = 16
MOD_ROWS = 8
N_MOD = 6
VMEM_LIMIT = 56 * 1024 * 1024
KEY_CHUNK = 512

BF16 = jnp.bfloat16
F32 = jnp.float32


def _dot(a, b):
    return jnp.dot(a, b, preferred_element_type=F32)


def _dot_nt(a, b_t):
    return lax.dot_general(a, b_t, (((1,), (1,)), ((), ())), preferred_element_type=F32)


def _dot_rows(a_ref, w, dot=_dot):
    rows = a_ref.shape[0]
    if rows <= MXU_ROWS:
        return dot(a_ref[...], w)
    return jnp.concatenate([dot(a_ref[r:r + MXU_ROWS, :], w) for r in range(0, rows, MXU_ROWS)], axis=0)


def _in_proj(h_ref, wt_ref):
    return _dot_rows(h_ref, wt_ref[...].astype(BF16), _dot_nt)


def _params(*sem):
    return pltpu.CompilerParams(dimension_semantics=sem, vmem_limit_bytes=VMEM_LIMIT)


def _mod_row(layer, seg, which):
    return (layer * MOD_ROWS + seg) * N_MOD + which


def _mod_kernel(cv_ref, w_ref, b_ref, o_ref):
    a = cv_ref[...]
    a = a * jax.nn.sigmoid(a)
    o_ref[...] = _dot(a.astype(BF16), w_ref[...].astype(BF16)) + b_ref[...]


def _modulation(cv, w_mod, b_mod):
    tn = 1024
    n_out = N_MOD * D_MODEL
    out = pl.pallas_call(
        _mod_kernel,
        grid=(DEPTH, n_out // tn),
        in_specs=[
            pl.BlockSpec((MOD_ROWS, D_MODEL), lambda l, j: (0, 0)),
            pl.BlockSpec((None, D_MODEL, tn), lambda l, j: (l, 0, j)),
            pl.BlockSpec((None, 1, tn), lambda l, j: (l, 0, j)),
        ],
        out_specs=pl.BlockSpec((None, MOD_ROWS, tn), lambda l, j: (l, 0, j)),
        out_shape=jax.ShapeDtypeStruct((DEPTH, MOD_ROWS, n_out), F32),
        compiler_params=_params("arbitrary", "arbitrary"),
        name="modulation",
    )(cv, w_mod, b_mod.reshape(DEPTH, 1, n_out))
    return out.reshape(DEPTH * MOD_ROWS * N_MOD, 1, D_MODEL)


def _norm_mod_kernel(x_ref, g_ref, sh_ref, sc_ref, o_ref):
    x = x_ref[...]
    y = x * lax.rsqrt(jnp.mean(x * x, axis=-1, keepdims=True) + EPS)
    y = y * g_ref[0]
    o_ref[...] = (y * (1 + sc_ref[0]) + sh_ref[0]).astype(BF16)


def _norm_mod(x, g, mods, seg_of_tile, layer, which_shift, tm):
    rows = x.shape[0]
    return pl.pallas_call(
        _norm_mod_kernel,
        grid=(rows // tm,),
        in_specs=[
            pl.BlockSpec((tm, D_MODEL), lambda i: (i, 0)),
            pl.BlockSpec((1, 1, D_MODEL), lambda i: (layer, 0, 0)),
            pl.BlockSpec((1, 1, D_MODEL), lambda i: (_mod_row(layer, seg_of_tile(i), which_shift), 0, 0)),
            pl.BlockSpec((1, 1, D_MODEL), lambda i: (_mod_row(layer, seg_of_tile(i), which_shift + 1), 0, 0)),
        ],
        out_specs=pl.BlockSpec((tm, D_MODEL), lambda i: (i, 0)),
        out_shape=jax.ShapeDtypeStruct((rows, D_MODEL), BF16),
        compiler_params=_params("arbitrary"),
        name="norm_mod",
    )(x, g.reshape(DEPTH, 1, D_MODEL), mods, mods)


def _branch_a_kernel(h_ref, *refs, seq):
    n_win = len(POOL_WINDOWS)
    wa_refs = refs[:n_win]
    pw_ref, ps_ref, o_ref, ext_ref = refs[n_win:]
    n_ext = seq + 2 * POOL_PAD
    zeros = jnp.zeros((POOL_PAD, POOL_GROUP), F32)
    ext_ref[0:POOL_PAD, :] = zeros
    ext_ref[seq + POOL_PAD:n_ext, :] = zeros
    t = lax.broadcasted_iota(jnp.int32, (seq, 1), 0)
    for g, w in enumerate(POOL_WINDOWS):
        cols = slice(g * POOL_GROUP, (g + 1) * POOL_GROUP)
        a = _in_proj(h_ref, wa_refs[g])
        ext_ref[POOL_PAD:seq + POOL_PAD, :] = a
        e = ext_ref[...]
        c = e + pltpu.roll(e, 1, 0)
        half = 1
        while 2 * half < w:
            c = pltpu.roll(c, half, 0) + pltpu.roll(c, n_ext - half, 0)
            half *= 2
        cnt = (jnp.minimum(t + w // 2, seq) - jnp.maximum(t - w // 2, 0)).astype(F32)
        pooled = c[POOL_PAD:seq + POOL_PAD, :] / cnt - a
        mixed = _dot(pooled.astype(BF16), pw_ref[g].astype(BF16)) * ps_ref[0, :, cols]
        o_ref[:, cols] = mixed.astype(BF16)


def _branch_a(h, w_in_t, pool_w, pool_scale, layer, seq):
    rows = h.shape[0]
    n_win = len(POOL_WINDOWS)
    group_rows = lambda g: pl.BlockSpec((None, POOL_GROUP, D_MODEL), lambda b: (layer, OFF_A // POOL_GROUP + g, 0))
    return pl.pallas_call(
        functools.partial(_branch_a_kernel, seq=seq),
        grid=(rows // seq,),
        in_specs=[pl.BlockSpec((seq, D_MODEL), lambda b: (b, 0))] + [group_rows(g) for g in range(n_win)] + [
            pl.BlockSpec((None, n_win, POOL_GROUP, POOL_GROUP), lambda b: (layer, 0, 0, 0)),
            pl.BlockSpec((1, 1, POOL_WIDTH), lambda b: (layer, 0, 0)),
        ],
        out_specs=pl.BlockSpec((seq, POOL_WIDTH), lambda b: (b, 0)),
        out_shape=jax.ShapeDtypeStruct((rows, POOL_WIDTH), BF16),
        scratch_shapes=[pltpu.VMEM((seq + 2 * POOL_PAD, POOL_GROUP), F32)],
        compiler_params=_params("arbitrary"),
        name="branch_a",
    )(h, *([w_in_t] * n_win), pool_w, pool_scale.reshape(DEPTH, 1, POOL_WIDTH))


def _conv3(u, cw_ref, seq):
    rows = u.shape[0]
    t = lax.broadcasted_iota(jnp.int32, (rows, 1), 0) % seq
    prev = jnp.where(t == 0, 0.0, pltpu.roll(u, 1, 0))
    nxt = jnp.where(t == seq - 1, 0.0, pltpu.roll(u, rows - 1, 0))
    return prev * cw_ref[0:1, :] + u * cw_ref[1:2, :] + nxt * cw_ref[2:3, :]


def _branch_b_kernel(h_ref, wgb_ref, wgc_ref, wx_ref, cw_ref, o_ref, *, seq):
    u = _in_proj(h_ref, wgc_ref) * _in_proj(h_ref, wx_ref)
    o_ref[...] = (_in_proj(h_ref, wgb_ref) * _conv3(u, cw_ref, seq)).astype(BF16)


def _branch_b(h, w_in_t, conv_w, layer, seq, tm, tn):
    rows = h.shape[0]
    nb = CONV_WIDTH // tn
    first = OFF_B // tn
    w_spec = lambda k: pl.BlockSpec((None, tn, D_MODEL), lambda b, c: (layer, first + k * nb + c, 0))
    return pl.pallas_call(
        functools.partial(_branch_b_kernel, seq=seq),
        grid=(rows // tm, nb),
        in_specs=[
            pl.BlockSpec((tm, D_MODEL), lambda b, c: (b, 0)),
            w_spec(0), w_spec(1), w_spec(2),
            pl.BlockSpec((None, 3, tn), lambda b, c: (layer, 0, c)),
        ],
        out_specs=pl.BlockSpec((tm, tn), lambda b, c: (b, c)),
        out_shape=jax.ShapeDtypeStruct((rows, CONV_WIDTH), BF16),
        compiler_params=_params("arbitrary", "arbitrary"),
        name="branch_b",
    )(h, w_in_t, w_in_t, w_in_t, conv_w)


def _lo_lanes():
    return lax.broadcasted_iota(jnp.int32, (1, LANES), 1) < QK_ROPE


def _q_proj_kernel(h_ref, wq_ref, lg_ref, wuq_ref, gn_ref, g2_ref, cs_ref, o_ref):
    zq = _in_proj(h_ref, wq_ref)
    cq = zq * lax.rsqrt(jnp.mean(zq * zq, axis=-1, keepdims=True) + EPS) * lg_ref[0]
    cq = cq.astype(BF16)
    gain_table = g2_ref[0] * cs_ref[...]
    lo = _lo_lanes()
    for pair in range(N_HEADS // 2):
        y = _dot(cq, wuq_ref[pair])
        nope = (y[:, 0:LANES], y[:, LANES:2 * LANES])
        yr, ys = y[:, 2 * LANES:3 * LANES], y[:, 3 * LANES:]
        sq_r = yr * yr
        rot = yr * gain_table[:, :LANES] + ys * gain_table[:, LANES:]
        for par in range(2):
            sq = nope[par] * nope[par] + jnp.where(lo if par == 0 else ~lo, sq_r, 0.0)
            r = lax.rsqrt(jnp.sum(sq, axis=-1, keepdims=True) / QK_DIM + EPS)
            hd = 2 * pair + par
            o_ref[0, hd, :, 0:LANES] = (nope[par] * r * gn_ref[0]).astype(BF16)
            o_ref[0, hd, :, LANES:QK_STORE] = jnp.where(lo if par == 0 else ~lo, rot * r, 0.0).astype(BF16)


def _q_proj(h, w_in_t, lora_g, wuq, gn, g2, cs, layer, batch, seq, tm):
    nt = seq // tm
    vec = lambda width: pl.BlockSpec((1, 1, width), lambda i: (layer, 0, 0))
    return pl.pallas_call(
        _q_proj_kernel,
        grid=(batch * nt,),
        in_specs=[
            pl.BlockSpec((tm, D_MODEL), lambda i: (i, 0)),
            pl.BlockSpec((None, Q_LORA, D_MODEL), lambda i: (layer, OFF_Q // Q_LORA, 0)),
            vec(Q_LORA),
            pl.BlockSpec((None, N_HEADS // 2, Q_LORA, 4 * LANES), lambda i: (layer, 0, 0, 0)),
            vec(LANES), vec(2 * LANES),
            pl.BlockSpec((tm, 2 * LANES), lambda i: (i % nt, 0)),
        ],
        out_specs=pl.BlockSpec((1, N_HEADS, tm, QK_STORE), lambda i: (i // nt, 0, i % nt, 0)),
        out_shape=jax.ShapeDtypeStruct((batch, N_HEADS, seq, QK_STORE), BF16),
        compiler_params=_params("arbitrary"),
        name="q_proj",
    )(h, w_in_t, lora_g.reshape(DEPTH, 1, Q_LORA), wuq, gn, g2, cs)


def _kv_proj_kernel(h_ref, wkv_ref, wkr_ref, lg_ref, wukv_ref, gn_ref, g2_ref, cs_ref, k_ref, v_ref):
    zc = _in_proj(h_ref, wkv_ref)
    q4 = QK_ROPE // 4
    wkr = jnp.concatenate([wkr_ref[...]] + [wkr_ref[s * q4:(s + 1) * q4, :] for s in (1, 0, 3, 2)], axis=0)
    kr = _dot_nt(h_ref[...], wkr.astype(BF16))
    ckv = zc * lax.rsqrt(jnp.mean(zc * zc, axis=-1, keepdims=True) + EPS) * lg_ref[0]
    ckv = ckv.astype(BF16)
    lo = _lo_lanes()
    ss_rope = jnp.sum(jnp.where(lo, kr * kr, 0.0), axis=-1, keepdims=True)
    u = kr * (g2_ref[0] * cs_ref[...])
    rot = u + pltpu.roll(u, LANES // 2, 1)
    rope = (jnp.where(lo, rot, 0.0), jnp.where(lo, 0.0, rot))
    ones = jnp.ones((h_ref.shape[0], LANES), BF16)
    for hd in range(N_HEADS):
        kv = _dot(ckv, wukv_ref[:, hd * 2 * LANES:(hd + 1) * 2 * LANES])
        kn, v = kv[:, :QK_NOPE], kv[:, QK_NOPE:]
        ss = jnp.sum(kn * kn, axis=-1, keepdims=True) + ss_rope
        r = lax.rsqrt(ss / QK_DIM + EPS)
        k_ref[0, hd, :, 0:LANES] = (kn * r * gn_ref[0]).astype(BF16)
        k_ref[0, hd, :, LANES:QK_STORE] = (rope[hd % 2] * r).astype(BF16)
        v_ref[0, hd, :, 0:V_DIM] = v.astype(BF16)
        v_ref[0, hd, :, V_DIM:2 * V_DIM] = ones


def _kv_proj(h, w_in_t, lora_g, wukv, gn, g2, cs, layer, batch, seq, tm):
    nt = seq // tm
    vec = lambda width: pl.BlockSpec((1, 1, width), lambda i: (layer, 0, 0))
    return pl.pallas_call(
        _kv_proj_kernel,
        grid=(batch * nt,),
        in_specs=[
            pl.BlockSpec((tm, D_MODEL), lambda i: (i, 0)),
            pl.BlockSpec((None, KV_LORA, D_MODEL), lambda i: (layer, OFF_KV // KV_LORA, 0)),
            pl.BlockSpec((None, QK_ROPE, D_MODEL), lambda i: (layer, OFF_KROPE // QK_ROPE, 0)),
            vec(KV_LORA),
            pl.BlockSpec((None, KV_LORA, N_HEADS * 2 * LANES), lambda i: (layer, 0, 0)),
            vec(LANES), vec(LANES),
            pl.BlockSpec((tm, LANES), lambda i: (i % nt, 0)),
        ],
        out_specs=[
            pl.BlockSpec((1, N_HEADS, tm, QK_STORE), lambda i: (i // nt, 0, i % nt, 0)),
            pl.BlockSpec((1, N_HEADS, tm, 2 * V_DIM), lambda i: (i // nt, 0, i % nt, 0)),
        ],
        out_shape=[
            jax.ShapeDtypeStruct((batch, N_HEADS, seq, QK_STORE), BF16),
            jax.ShapeDtypeStruct((batch, N_HEADS, seq, 2 * V_DIM), BF16),
        ],
        compiler_params=_params("arbitrary"),
        name="kv_proj",
    )(h, w_in_t, w_in_t, lora_g.reshape(DEPTH, 1, KV_LORA), wukv, gn, g2, cs)


def _attn_kernel(q_ref, *refs, tq):
    o_ref = refs[-1]
    kv_refs = refs[:-1]
    scale = ATTN_SCALE * math.log2(math.e)
    for hd in range(q_ref.shape[1]):
        for r in range(0, q_ref.shape[2], tq):
            q = q_ref[0, hd, r:r + tq, :]
            m = acc = None
            for s in range(0, len(kv_refs), 2):
                k_ref, v_ref = kv_refs[s], kv_refs[s + 1]
                n_keys = k_ref.shape[2]
                tk = min(KEY_CHUNK, n_keys)
                for j in range(n_keys // tk):
                    k = k_ref[0, hd, j * tk:(j + 1) * tk, :]
                    v = v_ref[0, hd, j * tk:(j + 1) * tk, :]
                    sc = _dot_nt(q, k)
                    m_chunk = jnp.max(sc, axis=-1, keepdims=True)
                    if m is None:
                        m = m_chunk
                        acc = _dot(jnp.exp2((sc - m) * scale).astype(BF16), v)
                    else:
                        m_new = jnp.maximum(m, m_chunk)
                        alpha = jnp.exp2((m - m_new) * scale)
                        acc = alpha * acc + _dot(jnp.exp2((sc - m_new) * scale).astype(BF16), v)
                        m = m_new
            o_ref[0, r:r + tq, hd * V_DIM:(hd + 1) * V_DIM] = (acc[:, :V_DIM] / acc[:, V_DIM:]).astype(BF16)


def _attention(q, kvs, tq, heads_per_step):
    batch, _, seq, _ = q.shape
    hps = heads_per_step
    head_spec = lambda rows, width: pl.BlockSpec((1, hps, rows, width), lambda b, hg: (b, hg, 0, 0))
    in_specs = [head_spec(seq, QK_STORE)]
    args = [q]
    for k, v in kvs:
        in_specs += [head_spec(k.shape[2], QK_STORE), head_spec(v.shape[2], 2 * V_DIM)]
        args += [k, v]
    return pl.pallas_call(
        functools.partial(_attn_kernel, tq=tq),
        grid=(batch, N_HEADS // hps),
        in_specs=in_specs,
        out_specs=pl.BlockSpec((1, seq, hps * V_DIM), lambda b, hg: (b, 0, hg)),
        out_shape=jax.ShapeDtypeStruct((batch, seq, N_HEADS * V_DIM), BF16),
        compiler_params=_params("arbitrary", "arbitrary"),
        name="attention",
    )(*args)


def _merge_kernel(h_ref, pa_ref, pb_ref, oc_ref, wg0_ref, wg1_ref, wg2_ref, wa_ref, wb_ref, wc_ref, o_ref):
    gated = lambda wg_ref, y_ref, w_ref: (jax.nn.sigmoid(_dot_rows(h_ref, wg_ref[...], _dot_nt))
                                          * _dot_rows(y_ref, w_ref[...].astype(BF16)))
    merged = gated(wg0_ref, pa_ref, wa_ref)
    merged = merged + gated(wg1_ref, pb_ref, wb_ref)
    merged = merged + gated(wg2_ref, oc_ref, wc_ref)
    o_ref[...] = merged.astype(BF16)


def _merge(h, pa, pb, oc, wg_t, wba, wbb, wbc, layer, tm, tn):
    rows = h.shape[0]
    nb = D_MODEL // tn
    row_spec = lambda width: pl.BlockSpec((tm, width), lambda i, j: (i, 0))
    gate_spec = lambda k: pl.BlockSpec((None, tn, D_MODEL), lambda i, j: (layer, k * nb + j, 0))
    col_spec = lambda depth: pl.BlockSpec((None, depth, tn), lambda i, j: (layer, 0, j))
    return pl.pallas_call(
        _merge_kernel,
        grid=(rows // tm, nb),
        in_specs=[
            row_spec(D_MODEL), row_spec(POOL_WIDTH), row_spec(CONV_WIDTH), row_spec(N_HEADS * V_DIM),
            gate_spec(0), gate_spec(1), gate_spec(2),
            col_spec(POOL_WIDTH), col_spec(CONV_WIDTH), col_spec(N_HEADS * V_DIM),
        ],
        out_specs=pl.BlockSpec((tm, tn), lambda i, j: (i, j)),
        out_shape=jax.ShapeDtypeStruct((rows, D_MODEL), BF16),
        compiler_params=_params("arbitrary", "arbitrary"),
        name="merge",
    )(h, pa, pb, oc, wg_t, wg_t, wg_t, wba, wbb, wbc)


def _matmul_residual_kernel(a_ref, w_ref, res_ref, gate_ref, o_ref):
    o_ref[...] = res_ref[...] + gate_ref[0] * _dot_rows(a_ref, w_ref[...].astype(BF16))


def _matmul_residual(a, w, res, mods, seg_of_tile, layer, which_gate, tm, tn):
    rows, depth = a.shape
    return pl.pallas_call(
        _matmul_residual_kernel,
        grid=(rows // tm, D_MODEL // tn),
        in_specs=[
            pl.BlockSpec((tm, depth), lambda i, j: (i, 0)),
            pl.BlockSpec((None, depth, tn), lambda i, j: (layer, 0, j)),
            pl.BlockSpec((tm, tn), lambda i, j: (i, j)),
            pl.BlockSpec((1, 1, tn), lambda i, j: (_mod_row(layer, seg_of_tile(i), which_gate), 0, j)),
        ],
        out_specs=pl.BlockSpec((tm, tn), lambda i, j: (i, j)),
        out_shape=jax.ShapeDtypeStruct((rows, D_MODEL), F32),
        compiler_params=_params("arbitrary", "arbitrary"),
        name="matmul_residual",
    )(a, w, res, mods)


def _matmul_residual_norm_kernel(a_ref, w_ref, res_ref, gate_ref, g_ref, sh_ref, sc_ref, x_ref, h_ref):
    a = a_ref[...]
    for n in range(0, D_MODEL, MXU_ROWS):
        cols = slice(n, n + MXU_ROWS)
        x_ref[:, cols] = res_ref[:, cols] + gate_ref[0, :, cols] * _dot(a, w_ref[:, cols])
    x = x_ref[...]
    y = x * lax.rsqrt(jnp.mean(x * x, axis=-1, keepdims=True) + EPS) * g_ref[0]
    h_ref[...] = (y * (1 + sc_ref[0]) + sh_ref[0]).astype(BF16)


def _matmul_residual_norm(a, w, res, mods, g, seg_of_tile, layer, which_gate, which_shift, tm):
    rows, depth = a.shape
    mod = lambda which: pl.BlockSpec((1, 1, D_MODEL), lambda i: (_mod_row(layer, seg_of_tile(i), which), 0, 0))
    row_spec = lambda width: pl.BlockSpec((tm, width), lambda i: (i, 0))
    return pl.pallas_call(
        _matmul_residual_norm_kernel,
        grid=(rows // tm,),
        in_specs=[
            row_spec(depth),
            pl.BlockSpec((None, depth, D_MODEL), lambda i: (layer, 0, 0)),
            row_spec(D_MODEL),
            mod(which_gate),
            pl.BlockSpec((1, 1, D_MODEL), lambda i: (layer, 0, 0)),
            mod(which_shift), mod(which_shift + 1),
        ],
        out_specs=[row_spec(D_MODEL), row_spec(D_MODEL)],
        out_shape=[jax.ShapeDtypeStruct((rows, D_MODEL), F32), jax.ShapeDtypeStruct((rows, D_MODEL), BF16)],
        compiler_params=_params("arbitrary"),
        name="matmul_residual_norm",
    )(a, w, res, mods, g.reshape(DEPTH, 1, D_MODEL), mods, mods)


def _ffn_up_kernel(h_ref, wu_ref, wv_ref, cw_ref, o_ref, *, seq, sub):
    for n in range(0, o_ref.shape[1], sub):
        cols = slice(n, n + sub)
        cu = _conv3(_dot_rows(h_ref, wu_ref[:, cols].astype(BF16)), cw_ref.at[:, cols], seq)
        gate = _dot_rows(h_ref, wv_ref[:, cols].astype(BF16))
        o_ref[:, cols] = (cu * jax.nn.sigmoid(cu) * gate).astype(BF16)


def _ffn_up(h, w_up, conv, layer, seq, tm, tn):
    rows = h.shape[0]
    nb = D_FF // tn
    return pl.pallas_call(
        functools.partial(_ffn_up_kernel, seq=seq, sub=min(tn, 256)),
        grid=(rows // tm, nb),
        in_specs=[
            pl.BlockSpec((tm, D_MODEL), lambda b, f: (b, 0)),
            pl.BlockSpec((None, D_MODEL, tn), lambda b, f: (layer, 0, f)),
            pl.BlockSpec((None, D_MODEL, tn), lambda b, f: (layer, 0, nb + f)),
            pl.BlockSpec((None, 3, tn), lambda b, f: (layer, 0, f)),
        ],
        out_specs=pl.BlockSpec((tm, tn), lambda b, f: (b, f)),
        out_shape=jax.ShapeDtypeStruct((rows, D_FF), BF16),
        compiler_params=_params("arbitrary", "arbitrary"),
        name="ffn_up",
    )(h, w_up, w_up, conv)


_ROPE_SWAP = np.concatenate([np.arange(16, 32), np.arange(0, 16), np.arange(48, 64), np.arange(32, 48)])


def _rope_cos_sin(length):
    rows = length // GRID_W
    row = jnp.repeat(jnp.arange(rows, dtype=jnp.int32), GRID_W).astype(F32)
    col = jnp.tile(jnp.arange(GRID_W, dtype=jnp.int32), rows).astype(F32)
    inv = ROPE_THETA ** (-jnp.arange(0, ROPE_AXIS, 2, dtype=F32) / ROPE_AXIS)
    ang_r = row[:, None] * inv[None]
    ang_c = col[:, None] * inv[None]
    cr, sr, cc, sc = jnp.cos(ang_r), jnp.sin(ang_r), jnp.cos(ang_c), jnp.sin(ang_c)
    return jnp.concatenate([cr, cr, cc, cc], axis=-1), jnp.concatenate([-sr, sr, -sc, sc], axis=-1)


def _identity_cos_sin(length):
    return jnp.ones((length, QK_ROPE), F32), jnp.zeros((length, QK_ROPE), F32)


def _rope_tables(cos, sin):
    return jnp.concatenate([cos, cos, sin, sin], axis=-1), jnp.concatenate([cos, sin], axis=-1)


def _prepared_weights(w_in_t, w_uq, w_ukv, w_out, w_ffn_down, q_head_g, k_head_g):
    uq = jnp.transpose(w_uq, (0, 2, 1, 3)).reshape(DEPTH, N_HEADS // 2, 2, Q_LORA, QK_DIM)
    even, odd = uq[:, :, 0], uq[:, :, 1]
    swapped = lambda w: w[..., QK_NOPE:][..., _ROPE_SWAP]
    q_rope, k_rope_g = q_head_g[:, QK_NOPE:], k_head_g[:, QK_NOPE:]
    q_swap, k_swap = q_rope[:, _ROPE_SWAP], k_rope_g[:, _ROPE_SWAP]
    return dict(
        wg_t=w_in_t[:, OFF_G:, :].astype(BF16),
        w_down=w_ffn_down.astype(BF16),
        w_out=w_out.astype(BF16),
        wuq=jnp.concatenate([even[..., :QK_NOPE], odd[..., :QK_NOPE], even[..., QK_NOPE:], odd[..., QK_NOPE:],
                             swapped(even), swapped(odd)], axis=-1).astype(BF16),
        wukv=w_ukv.reshape(DEPTH, KV_LORA, N_HEADS * (QK_NOPE + V_DIM)).astype(BF16),
        q_gn=q_head_g[:, None, :QK_NOPE], k_gn=k_head_g[:, None, :QK_NOPE],
        q_g2=jnp.concatenate([q_rope, q_rope, q_swap, q_swap], axis=-1)[:, None],
        k_g2=jnp.concatenate([k_rope_g, k_swap], axis=-1)[:, None],
    )


def _tiles(batch, seq):
    rows = batch * seq
    return dict(
        tm=min(rows, 1024),
        seq_tm=seq if seq >= 1024 else rows,
        full_row_tm=min(rows, 512),
        proj_tm=min(seq, 512),
        tq=min(seq, 512),
        attn_heads=max(1, min(N_HEADS, 2048 // seq)),
    )


def kernel(x, c, ctx, c_ctx, norm1_g, norm2_g, w_mod, b_mod, w_in, pool_w, pool_scale, conv_w, q_lora_g, w_uq,
           kv_lora_g, w_ukv, q_head_g, k_head_g, w_branch_a, w_branch_b, w_branch_c, w_out, w_ffn_up, ffn_conv,
           w_ffn_down):
    batch, seq, _ = x.shape
    ctx_len = ctx.shape[1]
    assert batch + 1 <= MOD_ROWS
    cv = jnp.concatenate([c, c_ctx[None], jnp.zeros((MOD_ROWS - batch - 1, D_MODEL), F32)], axis=0)
    mods = _modulation(cv, w_mod, b_mod)
    w_in_t = jnp.swapaxes(w_in, 1, 2)
    pw = _prepared_weights(w_in_t, w_uq, w_ukv, w_out, w_ffn_down, q_head_g, k_head_g)

    def kv_proj(h, layer, cs, n_tok, tiles):
        return _kv_proj(h, w_in_t, kv_lora_g, pw['wukv'], pw['k_gn'], pw['k_g2'], cs[1], layer, batch, n_tok,
                        tiles['proj_tm'])

    def stream_layer(xs, layer, seg_of_tile, cs, n_tok, ctx_kv, tiles):
        tm, seq_tm = tiles['tm'], tiles['seq_tm']
        h = _norm_mod(xs, norm1_g, mods, seg_of_tile(tm), layer, 0, tm)
        pa = _branch_a(h, w_in_t, pool_w, pool_scale, layer, n_tok)
        pb = _branch_b(h, w_in_t, conv_w, layer, n_tok, seq_tm, 256)
        q = _q_proj(h, w_in_t, q_lora_g, pw['wuq'], pw['q_gn'], pw['q_g2'], cs[0], layer, batch, n_tok,
                    tiles['proj_tm'])
        kv = kv_proj(h, layer, cs, n_tok, tiles)
        kvs = [kv] if ctx_kv is None else [kv, ctx_kv]
        oc = _attention(q, kvs, tiles['tq'], tiles['attn_heads']).reshape(batch * n_tok, N_HEADS * V_DIM)
        merged = _merge(h, pa, pb, oc, pw['wg_t'], w_branch_a, w_branch_b, w_branch_c, layer, tm, 256)
        full_tm = tiles['full_row_tm']
        xs, h2 = _matmul_residual_norm(merged, pw['w_out'], xs, mods, norm2_g, seg_of_tile(full_tm), layer, 2, 3,
                                       full_tm)
        hidden = _ffn_up(h2, w_ffn_up, ffn_conv, layer, n_tok, seq_tm, 512)
        xs = _matmul_residual(hidden, pw['w_down'], xs, mods, seg_of_tile(tm), layer, 5, tm, 512)
        return xs, kv

    cs_x = _rope_tables(*_rope_cos_sin(seq))
    cs_c = _rope_tables(*_identity_cos_sin(ctx_len))
    x_tiles = _tiles(batch, seq)
    c_tiles = _tiles(batch, ctx_len)
    x_seg = lambda tm: (lambda i: i // (seq // tm))
    c_seg = lambda tm: (lambda i: batch)

    xs = x.reshape(batch * seq, D_MODEL)
    cx = ctx.reshape(batch * ctx_len, D_MODEL)
    for layer in range(DEPTH):
        if layer == DEPTH - 1:
            hc = _norm_mod(cx, norm1_g, mods, c_seg(c_tiles['tm']), layer, 0, c_tiles['tm'])
            ctx_kv = kv_proj(hc, layer, cs_c, ctx_len, c_tiles)
        else:
            cx, ctx_kv = stream_layer(cx, layer, c_seg, cs_c, ctx_len, None, c_tiles)
        xs, _ = stream_layer(xs, layer, x_seg, cs_x, seq, tuple(ctx_kv), x_tiles)
    return xs.reshape(batch, seq, D_MODEL)
```

```python
import functools
import math

import jax
import jax.numpy as jnp
import numpy as np
from jax import lax
from jax.experimental import pallas as pl
from jax.experimental.pallas import tpu as pltpu

D_MODEL = 2048
DEPTH = 2
GRID_W = 64
POOL_WINDOWS = (2, 4, 8, 16)
POOL_GROUP = D_MODEL // 8
POOL_WIDTH = POOL_GROUP * len(POOL_WINDOWS)
CONV_WIDTH = D_MODEL // 2
N_HEADS = 16
QK_NOPE = 128
QK_ROPE = 64
ROPE_AXIS = QK_ROPE // 2
V_DIM = 128
QK_DIM = QK_NOPE + QK_ROPE
Q_LORA = 512
KV_LORA = 512
ROPE_THETA = 10000.0
ATTN_SCALE = QK_DIM ** -0.5
D_FF = 5632
N_BRANCH = 3
EPS = 1e-6
OFF_A = 0
OFF_B = OFF_A + POOL_WIDTH
OFF_Q = OFF_B + 3 * CONV_WIDTH
OFF_KV = OFF_Q + Q_LORA
OFF_KROPE = OFF_KV + KV_LORA
OFF_G = OFF_KROPE + QK_ROPE

LANES = 128
QK_STORE = 2 * LANES
MXU_ROWS = 512
POOL_PAD = 16
MOD_ROWS = 8
N_MOD = 6
VMEM_LIMIT = 56 * 1024 * 1024
KEY_CHUNK = 512

BF16 = jnp.bfloat16
F32 = jnp.float32


def _dot(a, b):
    return jnp.dot(a, b, preferred_element_type=F32)


def _dot_nt(a, b_t):
    return lax.dot_general(a, b_t, (((1,), (1,)), ((), ())), preferred_element_type=F32)


def _dot_rows(a_ref, w, dot=_dot):
    rows = a_ref.shape[0]
    if rows <= MXU_ROWS:
        return dot(a_ref[...], w)
    return jnp.concatenate([dot(a_ref[r:r + MXU_ROWS, :], w) for r in range(0, rows, MXU_ROWS)], axis=0)


def _in_proj(h_ref, wt_ref):
    return _dot_rows(h_ref, wt_ref[...].astype(BF16), _dot_nt)


def _params(*sem):
    return pltpu.CompilerParams(dimension_semantics=sem, vmem_limit_bytes=VMEM_LIMIT)


def _mod_row(layer, seg, which):
    return (layer * MOD_ROWS + seg) * N_MOD + which


def _mod_kernel(cv_ref, w_ref, b_ref, o_ref):
    a = cv_ref[...]
    a = a * jax.nn.sigmoid(a)
    o_ref[...] = _dot(a.astype(BF16), w_ref[...].astype(BF16)) + b_ref[...]


def _modulation(cv, w_mod, b_mod):
    tn = 1024
    n_out = N_MOD * D_MODEL
    out = pl.pallas_call(
        _mod_kernel,
        grid=(DEPTH, n_out // tn),
        in_specs=[
            pl.BlockSpec((MOD_ROWS, D_MODEL), lambda l, j: (0, 0)),
            pl.BlockSpec((None, D_MODEL, tn), lambda l, j: (l, 0, j)),
            pl.BlockSpec((None, 1, tn), lambda l, j: (l, 0, j)),
        ],
        out_specs=pl.BlockSpec((None, MOD_ROWS, tn), lambda l, j: (l, 0, j)),
        out_shape=jax.ShapeDtypeStruct((DEPTH, MOD_ROWS, n_out), F32),
        compiler_params=_params("arbitrary", "arbitrary"),
        name="modulation",
    )(cv, w_mod, b_mod.reshape(DEPTH, 1, n_out))
    return out.reshape(DEPTH * MOD_ROWS * N_MOD, 1, D_MODEL)


def _norm_mod_kernel(x_ref, g_ref, sh_ref, sc_ref, o_ref):
    x = x_ref[...]
    y = x * lax.rsqrt(jnp.mean(x * x, axis=-1, keepdims=True) + EPS)
    y = y * g_ref[0]
    o_ref[...] = (y * (1 + sc_ref[0]) + sh_ref[0]).astype(BF16)


def _norm_mod(x, g, mods, seg_of_tile, layer, which_shift, tm):
    rows = x.shape[0]
    return pl.pallas_call(
        _norm_mod_kernel,
        grid=(rows // tm,),
        in_specs=[
            pl.BlockSpec((tm, D_MODEL), lambda i: (i, 0)),
            pl.BlockSpec((1, 1, D_MODEL), lambda i: (layer, 0, 0)),
            pl.BlockSpec((1, 1, D_MODEL), lambda i: (_mod_row(layer, seg_of_tile(i), which_shift), 0, 0)),
            pl.BlockSpec((1, 1, D_MODEL), lambda i: (_mod_row(layer, seg_of_tile(i), which_shift + 1), 0, 0)),
        ],
        out_specs=pl.BlockSpec((tm, D_MODEL), lambda i: (i, 0)),
        out_shape=jax.ShapeDtypeStruct((rows, D_MODEL), BF16),
        compiler_params=_params("arbitrary"),
        name="norm_mod",
    )(x, g.reshape(DEPTH, 1, D_MODEL), mods, mods)


def _branch_a_kernel(h_ref, *refs, seq):
    n_win = len(POOL_WINDOWS)
    wa_refs = refs[:n_win]
    pw_ref, ps_ref, o_ref, ext_ref = refs[n_win:]
    n_ext = seq + 2 * POOL_PAD
    zeros = jnp.zeros((POOL_PAD, POOL_GROUP), F32)
    ext_ref[0:POOL_PAD, :] = zeros
    ext_ref[seq + POOL_PAD:n_ext, :] = zeros
    t = lax.broadcasted_iota(jnp.int32, (seq, 1), 0)
    for g, w in enumerate(POOL_WINDOWS):
        cols = slice(g * POOL_GROUP, (g + 1) * POOL_GROUP)
        a = _in_proj(h_ref, wa_refs[g])
        ext_ref[POOL_PAD:seq + POOL_PAD, :] = a
        e = ext_ref[...]
        c = e + pltpu.roll(e, 1, 0)
        half = 1
        while 2 * half < w:
            c = pltpu.roll(c, half, 0) + pltpu.roll(c, n_ext - half, 0)
            half *= 2
        cnt = (jnp.minimum(t + w // 2, seq) - jnp.maximum(t - w // 2, 0)).astype(F32)
        pooled = c[POOL_PAD:seq + POOL_PAD, :] / cnt - a
        mixed = _dot(pooled.astype(BF16), pw_ref[g].astype(BF16)) * ps_ref[0, :, cols]
        o_ref[:, cols] = mixed.astype(BF16)


def _branch_a(h, w_in_t, pool_w, pool_scale, layer, seq):
    rows = h.shape[0]
    n_win = len(POOL_WINDOWS)
    group_rows = lambda g: pl.BlockSpec((None, POOL_GROUP, D_MODEL), lambda b: (layer, OFF_A // POOL_GROUP + g, 0))
    return pl.pallas_call(
        functools.partial(_branch_a_kernel, seq=seq),
        grid=(rows // seq,),
        in_specs=[pl.BlockSpec((seq, D_MODEL), lambda b: (b, 0))] + [group_rows(g) for g in range(n_win)] + [
            pl.BlockSpec((None, n_win, POOL_GROUP, POOL_GROUP), lambda b: (layer, 0, 0, 0)),
            pl.BlockSpec((1, 1, POOL_WIDTH), lambda b: (layer, 0, 0)),
        ],
        out_specs=pl.BlockSpec((seq, POOL_WIDTH), lambda b: (b, 0)),
        out_shape=jax.ShapeDtypeStruct((rows, POOL_WIDTH), BF16),
        scratch_shapes=[pltpu.VMEM((seq + 2 * POOL_PAD, POOL_GROUP), F32)],
        compiler_params=_params("arbitrary"),
        name="branch_a",
    )(h, *([w_in_t] * n_win), pool_w, pool_scale.reshape(DEPTH, 1, POOL_WIDTH))


def _conv3(u, cw_ref, seq):
    rows = u.shape[0]
    t = lax.broadcasted_iota(jnp.int32, (rows, 1), 0) % seq
    prev = jnp.where(t == 0, 0.0, pltpu.roll(u, 1, 0))
    nxt = jnp.where(t == seq - 1, 0.0, pltpu.roll(u, rows - 1, 0))
    return prev * cw_ref[0:1, :] + u * cw_ref[1:2, :] + nxt * cw_ref[2:3, :]


def _branch_b_kernel(h_ref, wgb_ref, wgc_ref, wx_ref, cw_ref, o_ref, *, seq):
    u = _in_proj(h_ref, wgc_ref) * _in_proj(h_ref, wx_ref)
    o_ref[...] = (_in_proj(h_ref, wgb_ref) * _conv3(u, cw_ref, seq)).astype(BF16)


def _branch_b(h, w_in_t, conv_w, layer, seq, tm, tn):
    rows = h.shape[0]
    nb = CONV_WIDTH // tn
    first = OFF_B // tn
    w_spec = lambda k: pl.BlockSpec((None, tn, D_MODEL), lambda b, c: (layer, first + k * nb + c, 0))
    return pl.pallas_call(
        functools.partial(_branch_b_kernel, seq=seq),
        grid=(rows // tm, nb),
        in_specs=[
            pl.BlockSpec((tm, D_MODEL), lambda b, c: (b, 0)),
            w_spec(0), w_spec(1), w_spec(2),
            pl.BlockSpec((None, 3, tn), lambda b, c: (layer, 0, c)),
        ],
        out_specs=pl.BlockSpec((tm, tn), lambda b, c: (b, c)),
        out_shape=jax.ShapeDtypeStruct((rows, CONV_WIDTH), BF16),
        compiler_params=_params("arbitrary", "arbitrary"),
        name="branch_b",
    )(h, w_in_t, w_in_t, w_in_t, conv_w)


def _lo_lanes():
    return lax.broadcasted_iota(jnp.int32, (1, LANES), 1) < QK_ROPE


def _q_proj_kernel(h_ref, wq_ref, lg_ref, wuq_ref, gn_ref, g2_ref, cs_ref, o_ref):
    zq = _in_proj(h_ref, wq_ref)
    cq = zq * lax.rsqrt(jnp.mean(zq * zq, axis=-1, keepdims=True) + EPS) * lg_ref[0]
    cq = cq.astype(BF16)
    gain_table = g2_ref[0] * cs_ref[...]
    lo = _lo_lanes()
    for pair in range(N_HEADS // 2):
        y = _dot(cq, wuq_ref[pair])
        nope = (y[:, 0:LANES], y[:, LANES:2 * LANES])
        yr, ys = y[:, 2 * LANES:3 * LANES], y[:, 3 * LANES:]
        sq_r = yr * yr
        rot = yr * gain_table[:, :LANES] + ys * gain_table[:, LANES:]
        for par in range(2):
            sq = nope[par] * nope[par] + jnp.where(lo if par == 0 else ~lo, sq_r, 0.0)
            r = lax.rsqrt(jnp.sum(sq, axis=-1, keepdims=True) / QK_DIM + EPS)
            hd = 2 * pair + par
            o_ref[0, hd, :, 0:LANES] = (nope[par] * r * gn_ref[0]).astype(BF16)
            o_ref[0, hd, :, LANES:QK_STORE] = jnp.where(lo if par == 0 else ~lo, rot * r, 0.0).astype(BF16)


def _q_proj(h, w_in_t, lora_g, wuq, gn, g2, cs, layer, batch, seq, tm):
    nt = seq // tm
    vec = lambda width: pl.BlockSpec((1, 1, width), lambda i: (layer, 0, 0))
    return pl.pallas_call(
        _q_proj_kernel,
        grid=(batch * nt,),
        in_specs=[
            pl.BlockSpec((tm, D_MODEL), lambda i: (i, 0)),
            pl.BlockSpec((None, Q_LORA, D_MODEL), lambda i: (layer, OFF_Q // Q_LORA, 0)),
            vec(Q_LORA),
            pl.BlockSpec((None, N_HEADS // 2, Q_LORA, 4 * LANES), lambda i: (layer, 0, 0, 0)),
            vec(LANES), vec(2 * LANES),
            pl.BlockSpec((tm, 2 * LANES), lambda i: (i % nt, 0)),
        ],
        out_specs=pl.BlockSpec((1, N_HEADS, tm, QK_STORE), lambda i: (i // nt, 0, i % nt, 0)),
        out_shape=jax.ShapeDtypeStruct((batch, N_HEADS, seq, QK_STORE), BF16),
        compiler_params=_params("arbitrary"),
        name="q_proj",
    )(h, w_in_t, lora_g.reshape(DEPTH, 1, Q_LORA), wuq, gn, g2, cs)


def _kv_proj_kernel(h_ref, wkv_ref, wkr_ref, lg_ref, wukv_ref, gn_ref, g2_ref, cs_ref, k_ref, v_ref):
    zc = _in_proj(h_ref, wkv_ref)
    q4 = QK_ROPE // 4
    wkr = jnp.concatenate([wkr_ref[...]] + [wkr_ref[s * q4:(s + 1) * q4, :] for s in (1, 0, 3, 2)], axis=0)
    kr = _dot_nt(h_ref[...], wkr.astype(BF16))
    ckv = zc * lax.rsqrt(jnp.mean(zc * zc, axis=-1, keepdims=True) + EPS) * lg_ref[0]
    ckv = ckv.astype(BF16)
    lo = _lo_lanes()
    ss_rope = jnp.sum(jnp.where(lo, kr * kr, 0.0), axis=-1, keepdims=True)
    u = kr * (g2_ref[0] * cs_ref[...])
    rot = u + pltpu.roll(u, LANES // 2, 1)
    rope = (jnp.where(lo, rot, 0.0), jnp.where(lo, 0.0, rot))
    for hd in range(N_HEADS):
        kv = _dot(ckv, wukv_ref[:, hd * 2 * LANES:(hd + 1) * 2 * LANES])
        kn, v = kv[:, :QK_NOPE], kv[:, QK_NOPE:]
        ss = jnp.sum(kn * kn, axis=-1, keepdims=True) + ss_rope
        r = lax.rsqrt(ss / QK_DIM + EPS)
        k_ref[0, hd, :, 0:LANES] = (kn * r * gn_ref[0]).astype(BF16)
        k_ref[0, hd, :, LANES:QK_STORE] = (rope[hd % 2] * r).astype(BF16)
        v_ref[0, hd] = v.astype(BF16)


def _kv_proj(h, w_in_t, lora_g, wukv, gn, g2, cs, layer, batch, seq, tm):
    nt = seq // tm
    vec = lambda width: pl.BlockSpec((1, 1, width), lambda i: (layer, 0, 0))
    return pl.pallas_call(
        _kv_proj_kernel,
        grid=(batch * nt,),
        in_specs=[
            pl.BlockSpec((tm, D_MODEL), lambda i: (i, 0)),
            pl.BlockSpec((None, KV_LORA, D_MODEL), lambda i: (layer, OFF_KV // KV_LORA, 0)),
            pl.BlockSpec((None, QK_ROPE, D_MODEL), lambda i: (layer, OFF_KROPE // QK_ROPE, 0)),
            vec(KV_LORA),
            pl.BlockSpec((None, KV_LORA, N_HEADS * 2 * LANES), lambda i: (layer, 0, 0)),
            vec(LANES), vec(LANES),
            pl.BlockSpec((tm, LANES), lambda i: (i % nt, 0)),
        ],
        out_specs=[
            pl.BlockSpec((1, N_HEADS, tm, QK_STORE), lambda i: (i // nt, 0, i % nt, 0)),
            pl.BlockSpec((1, N_HEADS, tm, V_DIM), lambda i: (i // nt, 0, i % nt, 0)),
        ],
        out_shape=[
            jax.ShapeDtypeStruct((batch, N_HEADS, seq, QK_STORE), BF16),
            jax.ShapeDtypeStruct((batch, N_HEADS, seq, V_DIM), BF16),
        ],
        compiler_params=_params("arbitrary"),
        name="kv_proj",
    )(h, w_in_t, w_in_t, lora_g.reshape(DEPTH, 1, KV_LORA), wukv, gn, g2, cs)


def _attn_kernel(q_ref, *refs, tq):
    o_ref = refs[-1]
    kv_refs = refs[:-1]
    scale = ATTN_SCALE * math.log2(math.e)
    steps = []
    for s in range(0, len(kv_refs), 2):
        n_keys = kv_refs[s].shape[2]
        tk = min(KEY_CHUNK, n_keys)
        steps += [(kv_refs[s], kv_refs[s + 1], j, tk) for j in range(0, n_keys, tk)]
    for hd in range(q_ref.shape[1]):
        for r in range(0, q_ref.shape[2], tq):
            q = q_ref[0, hd, r:r + tq, :]
            m = acc = None
            for k_ref, v_ref, j, n in steps:
                sc = _dot_nt(q, k_ref[0, hd, j:j + n, :])
                m_new = jnp.max(sc, axis=-1, keepdims=True)
                if m is not None:
                    m_new = jnp.maximum(m, m_new)
                p = jnp.exp2((sc - m_new) * scale).astype(BF16)
                pv = _dot(p, jnp.concatenate([v_ref[0, hd, j:j + n, :], jnp.ones((n, LANES), BF16)], axis=1))
                acc = pv if m is None else jnp.exp2((m - m_new) * scale) * acc + pv
                m = m_new
            o_ref[0, r:r + tq, hd * V_DIM:(hd + 1) * V_DIM] = (acc[:, :V_DIM] / acc[:, V_DIM:]).astype(BF16)


def _attention(q, kvs, tq, heads_per_step):
    batch, _, seq, _ = q.shape
    hps = heads_per_step
    head_spec = lambda rows, width: pl.BlockSpec((1, hps, rows, width), lambda b, hg: (b, hg, 0, 0))
    in_specs = [head_spec(seq, QK_STORE)]
    args = [q]
    for k, v in kvs:
        in_specs += [head_spec(k.shape[2], QK_STORE), head_spec(v.shape[2], V_DIM)]
        args += [k, v]
    return pl.pallas_call(
        functools.partial(_attn_kernel, tq=tq),
        grid=(batch, N_HEADS // hps),
        in_specs=in_specs,
        out_specs=pl.BlockSpec((1, seq, hps * V_DIM), lambda b, hg: (b, 0, hg)),
        out_shape=jax.ShapeDtypeStruct((batch, seq, N_HEADS * V_DIM), BF16),
        compiler_params=_params("arbitrary", "arbitrary"),
        name="attention",
    )(*args)


def _merge_kernel(h_ref, pa_ref, pb_ref, oc_ref, wg0_ref, wg1_ref, wg2_ref, wa_ref, wb_ref, wc_ref, o_ref):
    branches = ((wg0_ref, pa_ref, wa_ref), (wg1_ref, pb_ref, wb_ref), (wg2_ref, oc_ref, wc_ref))
    weights = [w_ref[...].astype(BF16) for _, _, w_ref in branches]
    for r in range(0, o_ref.shape[0], MXU_ROWS):
        rows = slice(r, r + MXU_ROWS)
        merged = None
        for (wg_ref, y_ref, _), w in zip(branches, weights):
            term = jax.nn.sigmoid(_dot_nt(h_ref[rows, :], wg_ref[...])) * _dot(y_ref[rows, :], w)
            merged = term if merged is None else merged + term
        o_ref[rows, :] = merged.astype(BF16)


def _merge(h, pa, pb, oc, wg_t, wba, wbb, wbc, layer, tm, tn):
    rows = h.shape[0]
    nb = D_MODEL // tn
    row_spec = lambda width: pl.BlockSpec((tm, width), lambda i, j: (i, 0))
    gate_spec = lambda k: pl.BlockSpec((None, tn, D_MODEL), lambda i, j: (layer, k * nb + j, 0))
    col_spec = lambda depth: pl.BlockSpec((None, depth, tn), lambda i, j: (layer, 0, j))
    return pl.pallas_call(
        _merge_kernel,
        grid=(rows // tm, nb),
        in_specs=[
            row_spec(D_MODEL), row_spec(POOL_WIDTH), row_spec(CONV_WIDTH), row_spec(N_HEADS * V_DIM),
            gate_spec(0), gate_spec(1), gate_spec(2),
            col_spec(POOL_WIDTH), col_spec(CONV_WIDTH), col_spec(N_HEADS * V_DIM),
        ],
        out_specs=pl.BlockSpec((tm, tn), lambda i, j: (i, j)),
        out_shape=jax.ShapeDtypeStruct((rows, D_MODEL), BF16),
        compiler_params=_params("arbitrary", "arbitrary"),
        name="merge",
    )(h, pa, pb, oc, wg_t, wg_t, wg_t, wba, wbb, wbc)


def _matmul_residual_kernel(a_ref, w_ref, res_ref, gate_ref, o_ref):
    w = w_ref[...].astype(BF16)
    for r in range(0, o_ref.shape[0], MXU_ROWS):
        rows = slice(r, r + MXU_ROWS)
        o_ref[rows, :] = res_ref[rows, :] + gate_ref[0] * _dot(a_ref[rows, :], w)


def _matmul_residual(a, w, res, mods, seg_of_tile, layer, which_gate, tm, tn):
    rows, depth = a.shape
    return pl.pallas_call(
        _matmul_residual_kernel,
        grid=(rows // tm, D_MODEL // tn),
        in_specs=[
            pl.BlockSpec((tm, depth), lambda i, j: (i, 0)),
            pl.BlockSpec((None, depth, tn), lambda i, j: (layer, 0, j)),
            pl.BlockSpec((tm, tn), lambda i, j: (i, j)),
            pl.BlockSpec((1, 1, tn), lambda i, j: (_mod_row(layer, seg_of_tile(i), which_gate), 0, j)),
        ],
        out_specs=pl.BlockSpec((tm, tn), lambda i, j: (i, j)),
        out_shape=jax.ShapeDtypeStruct((rows, D_MODEL), F32),
        compiler_params=_params("arbitrary", "arbitrary"),
        name="matmul_residual",
    )(a, w, res, mods)


def _matmul_residual_norm_kernel(a_ref, w_ref, res_ref, gate_ref, g_ref, sh_ref, sc_ref, x_ref, h_ref):
    a = a_ref[...]
    for n in range(0, D_MODEL, MXU_ROWS):
        cols = slice(n, n + MXU_ROWS)
        x_ref[:, cols] = res_ref[:, cols] + gate_ref[0, :, cols] * _dot(a, w_ref[:, cols])
    x = x_ref[...]
    y = x * lax.rsqrt(jnp.mean(x * x, axis=-1, keepdims=True) + EPS) * g_ref[0]
    h_ref[...] = (y * (1 + sc_ref[0]) + sh_ref[0]).astype(BF16)


def _matmul_residual_norm(a, w, res, mods, g, seg_of_tile, layer, which_gate, which_shift, tm):
    rows, depth = a.shape
    mod = lambda which: pl.BlockSpec((1, 1, D_MODEL), lambda i: (_mod_row(layer, seg_of_tile(i), which), 0, 0))
    row_spec = lambda width: pl.BlockSpec((tm, width), lambda i: (i, 0))
    return pl.pallas_call(
        _matmul_residual_norm_kernel,
        grid=(rows // tm,),
        in_specs=[
            row_spec(depth),
            pl.BlockSpec((None, depth, D_MODEL), lambda i: (layer, 0, 0)),
            row_spec(D_MODEL),
            mod(which_gate),
            pl.BlockSpec((1, 1, D_MODEL), lambda i: (layer, 0, 0)),
            mod(which_shift), mod(which_shift + 1),
        ],
        out_specs=[row_spec(D_MODEL), row_spec(D_MODEL)],
        out_shape=[jax.ShapeDtypeStruct((rows, D_MODEL), F32), jax.ShapeDtypeStruct((rows, D_MODEL), BF16)],
        compiler_params=_params("arbitrary"),
        name="matmul_residual_norm",
    )(a, w, res, mods, g.reshape(DEPTH, 1, D_MODEL), mods, mods)


def _ffn_up_kernel(h_ref, wu_ref, wv_ref, cw_ref, o_ref, *, seq, sub):
    for n in range(0, o_ref.shape[1], sub):
        cols = slice(n, n + sub)
        cu = _conv3(_dot_rows(h_ref, wu_ref[:, cols].astype(BF16)), cw_ref.at[:, cols], seq)
        gate = _dot_rows(h_ref, wv_ref[:, cols].astype(BF16))
        o_ref[:, cols] = (cu * jax.nn.sigmoid(cu) * gate).astype(BF16)


def _ffn_up(h, w_up, conv, layer, seq, tm, tn):
    rows = h.shape[0]
    nb = D_FF // tn
    return pl.pallas_call(
        functools.partial(_ffn_up_kernel, seq=seq, sub=min(tn, 256)),
        grid=(rows // tm, nb),
        in_specs=[
            pl.BlockSpec((tm, D_MODEL), lambda b, f: (b, 0)),
            pl.BlockSpec((None, D_MODEL, tn), lambda b, f: (layer, 0, f)),
            pl.BlockSpec((None, D_MODEL, tn), lambda b, f: (layer, 0, nb + f)),
            pl.BlockSpec((None, 3, tn), lambda b, f: (layer, 0, f)),
        ],
        out_specs=pl.BlockSpec((tm, tn), lambda b, f: (b, f)),
        out_shape=jax.ShapeDtypeStruct((rows, D_FF), BF16),
        compiler_params=_params("arbitrary", "arbitrary"),
        name="ffn_up",
    )(h, w_up, w_up, conv)


_ROPE_SWAP = np.concatenate([np.arange(16, 32), np.arange(0, 16), np.arange(48, 64), np.arange(32, 48)])


def _rope_cos_sin(length):
    rows = length // GRID_W
    row = np.repeat(np.arange(rows, dtype=np.int32), GRID_W).astype(np.float32)
    col = np.tile(np.arange(GRID_W, dtype=np.int32), rows).astype(np.float32)
    inv = np.float32(ROPE_THETA) ** (-np.arange(0, ROPE_AXIS, 2, dtype=np.float32) / np.float32(ROPE_AXIS))
    ang_r = row[:, None] * inv[None]
    ang_c = col[:, None] * inv[None]
    cr, sr, cc, sc = np.cos(ang_r), np.sin(ang_r), np.cos(ang_c), np.sin(ang_c)
    return np.concatenate([cr, cr, cc, cc], axis=-1), np.concatenate([-sr, sr, -sc, sc], axis=-1)


def _identity_cos_sin(length):
    return np.ones((length, QK_ROPE), np.float32), np.zeros((length, QK_ROPE), np.float32)


def _rope_tables(cos, sin):
    return (jnp.asarray(np.concatenate([cos, cos, sin, sin], axis=-1), F32),
            jnp.asarray(np.concatenate([cos, sin], axis=-1), F32))


def _prepared_weights(w_in_t, w_uq, w_ukv, w_out, w_ffn_down, q_head_g, k_head_g):
    uq = jnp.transpose(w_uq, (0, 2, 1, 3)).reshape(DEPTH, N_HEADS // 2, 2, Q_LORA, QK_DIM)
    even, odd = uq[:, :, 0], uq[:, :, 1]
    swapped = lambda w: w[..., QK_NOPE:][..., _ROPE_SWAP]
    q_rope, k_rope_g = q_head_g[:, QK_NOPE:], k_head_g[:, QK_NOPE:]
    q_swap, k_swap = q_rope[:, _ROPE_SWAP], k_rope_g[:, _ROPE_SWAP]
    return dict(
        wg_t=w_in_t[:, OFF_G:, :].astype(BF16),
        w_down=w_ffn_down.astype(BF16),
        w_out=w_out.astype(BF16),
        wuq=jnp.concatenate([even[..., :QK_NOPE], odd[..., :QK_NOPE], even[..., QK_NOPE:], odd[..., QK_NOPE:],
                             swapped(even), swapped(odd)], axis=-1).astype(BF16),
        wukv=w_ukv.reshape(DEPTH, KV_LORA, N_HEADS * (QK_NOPE + V_DIM)).astype(BF16),
        q_gn=q_head_g[:, None, :QK_NOPE], k_gn=k_head_g[:, None, :QK_NOPE],
        q_g2=jnp.concatenate([q_rope, q_rope, q_swap, q_swap], axis=-1)[:, None],
        k_g2=jnp.concatenate([k_rope_g, k_swap], axis=-1)[:, None],
    )


def _tiles(batch, seq):
    rows = batch * seq
    return dict(
        tm=min(rows, 1024),
        seq_tm=seq if seq >= 1024 else rows,
        full_row_tm=min(rows, 512),
        proj_tm=min(seq, 512),
        tq=min(seq, 512),
        attn_heads=max(1, min(N_HEADS, 2048 // seq)),
        narrow_tn=256,
        wide_tn=512,
    )


def kernel(x, c, ctx, c_ctx, norm1_g, norm2_g, w_mod, b_mod, w_in, pool_w, pool_scale, conv_w, q_lora_g, w_uq,
           kv_lora_g, w_ukv, q_head_g, k_head_g, w_branch_a, w_branch_b, w_branch_c, w_out, w_ffn_up, ffn_conv,
           w_ffn_down):
    batch, seq, _ = x.shape
    ctx_len = ctx.shape[1]
    assert batch + 1 <= MOD_ROWS
    cv = jnp.concatenate([c, c_ctx[None], jnp.zeros((MOD_ROWS - batch - 1, D_MODEL), F32)], axis=0)
    mods = _modulation(cv, w_mod, b_mod)
    w_in_t = jnp.swapaxes(w_in, 1, 2)
    pw = _prepared_weights(w_in_t, w_uq, w_ukv, w_out, w_ffn_down, q_head_g, k_head_g)

    def kv_proj(h, layer, cs, n_tok, tiles):
        return _kv_proj(h, w_in_t, kv_lora_g, pw['wukv'], pw['k_gn'], pw['k_g2'], cs[1], layer, batch, n_tok,
                        tiles['proj_tm'])

    def stream_layer(xs, layer, seg_of_tile, cs, n_tok, ctx_kv, tiles):
        tm, seq_tm = tiles['tm'], tiles['seq_tm']
        h = _norm_mod(xs, norm1_g, mods, seg_of_tile(tm), layer, 0, tm)
        pa = _branch_a(h, w_in_t, pool_w, pool_scale, layer, n_tok)
        pb = _branch_b(h, w_in_t, conv_w, layer, n_tok, seq_tm, tiles['narrow_tn'])
        q = _q_proj(h, w_in_t, q_lora_g, pw['wuq'], pw['q_gn'], pw['q_g2'], cs[0], layer, batch, n_tok,
                    tiles['proj_tm'])
        kv = kv_proj(h, layer, cs, n_tok, tiles)
        kvs = [kv] if ctx_kv is None else [kv, ctx_kv]
        oc = _attention(q, kvs, tiles['tq'], tiles['attn_heads']).reshape(batch * n_tok, N_HEADS * V_DIM)
        merged = _merge(h, pa, pb, oc, pw['wg_t'], w_branch_a, w_branch_b, w_branch_c, layer, tm,
                        tiles['narrow_tn'])
        full_tm = tiles['full_row_tm']
        xs, h2 = _matmul_residual_norm(merged, pw['w_out'], xs, mods, norm2_g, seg_of_tile(full_tm), layer, 2, 3,
                                       full_tm)
        hidden = _ffn_up(h2, w_ffn_up, ffn_conv, layer, n_tok, seq_tm, tiles['wide_tn'])
        xs = _matmul_residual(hidden, pw['w_down'], xs, mods, seg_of_tile(tm), layer, 5, tm, tiles['wide_tn'])
        return xs, kv

    cs_x = _rope_tables(*_rope_cos_sin(seq))
    cs_c = _rope_tables(*_identity_cos_sin(ctx_len))
    x_tiles = _tiles(batch, seq)
    c_tiles = _tiles(batch, ctx_len)
    x_seg = lambda tm: (lambda i: i // (seq // tm))
    c_seg = lambda tm: (lambda i: batch)

    xs = x.reshape(batch * seq, D_MODEL)
    cx = ctx.reshape(batch * ctx_len, D_MODEL)
    for layer in range(DEPTH):
        if layer == DEPTH - 1:
            hc = _norm_mod(cx, norm1_g, mods, c_seg(c_tiles['tm']), layer, 0, c_tiles['tm'])
            ctx_kv = kv_proj(hc, layer, cs_c, ctx_len, c_tiles)
        else:
            cx, ctx_kv = stream_layer(cx, layer, c_seg, cs_c, ctx_len, None, c_tiles)
        xs, _ = stream_layer(xs, layer, x_seg, cs_x, seq, tuple(ctx_kv), x_tiles)
    return xs.reshape(batch, seq, D_MODEL)
```

```python
import functools
import math

import jax
import jax.numpy as jnp
import numpy as np
from jax import lax
from jax.experimental import pallas as pl
from jax.experimental.pallas import tpu as pltpu

D_MODEL = 2048
DEPTH = 2
GRID_W = 64
POOL_WINDOWS = (2, 4, 8, 16)
POOL_GROUP = D_MODEL // 8
POOL_WIDTH = POOL_GROUP * len(POOL_WINDOWS)
CONV_WIDTH = D_MODEL // 2
N_HEADS = 16
QK_NOPE = 128
QK_ROPE = 64
ROPE_AXIS = QK_ROPE // 2
V_DIM = 128
QK_DIM = QK_NOPE + QK_ROPE
Q_LORA = 512
KV_LORA = 512
ROPE_THETA = 10000.0
ATTN_SCALE = QK_DIM ** -0.5
D_FF = 5632
N_BRANCH = 3
EPS = 1e-6
OFF_A = 0
OFF_B = OFF_A + POOL_WIDTH
OFF_Q = OFF_B + 3 * CONV_WIDTH
OFF_KV = OFF_Q + Q_LORA
OFF_KROPE = OFF_KV + KV_LORA
OFF_G = OFF_KROPE + QK_ROPE

LANES = 128
QK_STORE = 2 * LANES
MXU_ROWS = 512
PROJ_ROWS = 512
POOL_PAD = 16
MOD_ROWS = 8
N_MOD = 6
VMEM_LIMIT = 56 * 1024 * 1024
KEY_CHUNK = 512

BF16 = jnp.bfloat16
F32 = jnp.float32


def _dot(a, b):
    return jnp.dot(a, b, preferred_element_type=F32)


def _dot_nt(a, b_t):
    return lax.dot_general(a, b_t, (((1,), (1,)), ((), ())), preferred_element_type=F32)


def _dot_rows(a_ref, w, dot=_dot):
    rows = a_ref.shape[0]
    if rows <= MXU_ROWS:
        return dot(a_ref[...], w)
    return jnp.concatenate([dot(a_ref[r:r + MXU_ROWS, :], w) for r in range(0, rows, MXU_ROWS)], axis=0)


def _row_blocks(rows, block=MXU_ROWS):
    return [slice(r, min(r + block, rows)) for r in range(0, rows, block)]


def _in_proj(h_ref, wt_ref):
    return _dot_rows(h_ref, wt_ref[...].astype(BF16), _dot_nt)


def _params(*sem):
    return pltpu.CompilerParams(dimension_semantics=sem, vmem_limit_bytes=VMEM_LIMIT)


def _mod_row(layer, seg, which):
    return (layer * MOD_ROWS + seg) * N_MOD + which


def _mod_kernel(cv_ref, w_ref, b_ref, o_ref):
    a = cv_ref[...]
    a = a * jax.nn.sigmoid(a)
    o_ref[...] = _dot(a.astype(BF16), w_ref[...].astype(BF16)) + b_ref[...]


def _modulation(cv, w_mod, b_mod):
    tn = 1024
    n_out = N_MOD * D_MODEL
    out = pl.pallas_call(
        _mod_kernel,
        grid=(DEPTH, n_out // tn),
        in_specs=[
            pl.BlockSpec((MOD_ROWS, D_MODEL), lambda l, j: (0, 0)),
            pl.BlockSpec((None, D_MODEL, tn), lambda l, j: (l, 0, j)),
            pl.BlockSpec((None, 1, tn), lambda l, j: (l, 0, j)),
        ],
        out_specs=pl.BlockSpec((None, MOD_ROWS, tn), lambda l, j: (l, 0, j)),
        out_shape=jax.ShapeDtypeStruct((DEPTH, MOD_ROWS, n_out), F32),
        compiler_params=_params("arbitrary", "arbitrary"),
        name="modulation",
    )(cv, w_mod, b_mod.reshape(DEPTH, 1, n_out))
    return out.reshape(DEPTH * MOD_ROWS * N_MOD, 1, D_MODEL)


def _norm_mod_kernel(x_ref, g_ref, sh_ref, sc_ref, o_ref):
    x = x_ref[...]
    y = x * lax.rsqrt(jnp.mean(x * x, axis=-1, keepdims=True) + EPS)
    y = y * g_ref[0]
    o_ref[...] = (y * (1 + sc_ref[0]) + sh_ref[0]).astype(BF16)


def _norm_mod(x, g, mods, seg_of_tile, layer, which_shift, tm):
    rows = x.shape[0]
    return pl.pallas_call(
        _norm_mod_kernel,
        grid=(rows // tm,),
        in_specs=[
            pl.BlockSpec((tm, D_MODEL), lambda i: (i, 0)),
            pl.BlockSpec((1, 1, D_MODEL), lambda i: (layer, 0, 0)),
            pl.BlockSpec((1, 1, D_MODEL), lambda i: (_mod_row(layer, seg_of_tile(i), which_shift), 0, 0)),
            pl.BlockSpec((1, 1, D_MODEL), lambda i: (_mod_row(layer, seg_of_tile(i), which_shift + 1), 0, 0)),
        ],
        out_specs=pl.BlockSpec((tm, D_MODEL), lambda i: (i, 0)),
        out_shape=jax.ShapeDtypeStruct((rows, D_MODEL), BF16),
        compiler_params=_params("arbitrary"),
        name="norm_mod",
    )(x, g.reshape(DEPTH, 1, D_MODEL), mods, mods)


def _branch_a_kernel(h_ref, *refs, seq):
    n_win = len(POOL_WINDOWS)
    wa_refs = refs[:n_win]
    pw_ref, ps_ref, o_ref, ext_ref = refs[n_win:]
    n_ext = seq + 2 * POOL_PAD
    zeros = jnp.zeros((POOL_PAD, POOL_GROUP), F32)
    ext_ref[0:POOL_PAD, :] = zeros
    ext_ref[seq + POOL_PAD:n_ext, :] = zeros
    t = lax.broadcasted_iota(jnp.int32, (seq, 1), 0)
    for g, w in enumerate(POOL_WINDOWS):
        cols = slice(g * POOL_GROUP, (g + 1) * POOL_GROUP)
        a = _in_proj(h_ref, wa_refs[g])
        ext_ref[POOL_PAD:seq + POOL_PAD, :] = a
        e = ext_ref[...]
        c = e + pltpu.roll(e, 1, 0)
        half = 1
        while 2 * half < w:
            c = pltpu.roll(c, half, 0) + pltpu.roll(c, n_ext - half, 0)
            half *= 2
        cnt = (jnp.minimum(t + w // 2, seq) - jnp.maximum(t - w // 2, 0)).astype(F32)
        pooled = c[POOL_PAD:seq + POOL_PAD, :] / cnt - a
        mixed = _dot(pooled.astype(BF16), pw_ref[g].astype(BF16)) * ps_ref[0, :, cols]
        o_ref[:, cols] = mixed.astype(BF16)


def _branch_a(h, w_in_t, pool_w, pool_scale, layer, seq):
    rows = h.shape[0]
    n_win = len(POOL_WINDOWS)
    group_rows = lambda g: pl.BlockSpec((None, POOL_GROUP, D_MODEL), lambda b: (layer, OFF_A // POOL_GROUP + g, 0))
    return pl.pallas_call(
        functools.partial(_branch_a_kernel, seq=seq),
        grid=(rows // seq,),
        in_specs=[pl.BlockSpec((seq, D_MODEL), lambda b: (b, 0))] + [group_rows(g) for g in range(n_win)] + [
            pl.BlockSpec((None, n_win, POOL_GROUP, POOL_GROUP), lambda b: (layer, 0, 0, 0)),
            pl.BlockSpec((1, 1, POOL_WIDTH), lambda b: (layer, 0, 0)),
        ],
        out_specs=pl.BlockSpec((seq, POOL_WIDTH), lambda b: (b, 0)),
        out_shape=jax.ShapeDtypeStruct((rows, POOL_WIDTH), BF16),
        scratch_shapes=[pltpu.VMEM((seq + 2 * POOL_PAD, POOL_GROUP), F32)],
        compiler_params=_params("arbitrary"),
        name="branch_a",
    )(h, *([w_in_t] * n_win), pool_w, pool_scale.reshape(DEPTH, 1, POOL_WIDTH))


def _conv3(u, cw_ref, seq):
    rows = u.shape[0]
    t = lax.broadcasted_iota(jnp.int32, (rows, 1), 0) % seq
    prev = jnp.where(t == 0, 0.0, pltpu.roll(u, 1, 0))
    nxt = jnp.where(t == seq - 1, 0.0, pltpu.roll(u, rows - 1, 0))
    return prev * cw_ref[0:1, :] + u * cw_ref[1:2, :] + nxt * cw_ref[2:3, :]


def _branch_b_kernel(h_ref, wgb_ref, wgc_ref, wx_ref, cw_ref, o_ref, *, seq):
    u = _in_proj(h_ref, wgc_ref) * _in_proj(h_ref, wx_ref)
    o_ref[...] = (_in_proj(h_ref, wgb_ref) * _conv3(u, cw_ref, seq)).astype(BF16)


def _branch_b(h, w_in_t, conv_w, layer, seq, tm, tn):
    rows = h.shape[0]
    nb = CONV_WIDTH // tn
    first = OFF_B // tn
    w_spec = lambda k: pl.BlockSpec((None, tn, D_MODEL), lambda b, c: (layer, first + k * nb + c, 0))
    return pl.pallas_call(
        functools.partial(_branch_b_kernel, seq=seq),
        grid=(rows // tm, nb),
        in_specs=[
            pl.BlockSpec((tm, D_MODEL), lambda b, c: (b, 0)),
            w_spec(0), w_spec(1), w_spec(2),
            pl.BlockSpec((None, 3, tn), lambda b, c: (layer, 0, c)),
        ],
        out_specs=pl.BlockSpec((tm, tn), lambda b, c: (b, c)),
        out_shape=jax.ShapeDtypeStruct((rows, CONV_WIDTH), BF16),
        compiler_params=_params("arbitrary", "arbitrary"),
        name="branch_b",
    )(h, w_in_t, w_in_t, w_in_t, conv_w)


def _lo_lanes():
    return lax.broadcasted_iota(jnp.int32, (1, LANES), 1) < QK_ROPE


def _q_proj_kernel(h_ref, wq_ref, lg_ref, wuq_ref, gn_ref, g2_ref, cs_ref, o_ref):
    wq = wq_ref[...].astype(BF16)
    lo = _lo_lanes()
    for rows in _row_blocks(h_ref.shape[0], PROJ_ROWS):
        zq = _dot_nt(h_ref[rows, :], wq)
        cq = zq * lax.rsqrt(jnp.mean(zq * zq, axis=-1, keepdims=True) + EPS) * lg_ref[0]
        cq = cq.astype(BF16)
        gain_table = g2_ref[0] * cs_ref[rows, :]
        for pair in range(N_HEADS // 2):
            y = _dot(cq, wuq_ref[pair])
            nope = (y[:, 0:LANES], y[:, LANES:2 * LANES])
            yr, ys = y[:, 2 * LANES:3 * LANES], y[:, 3 * LANES:]
            sq_r = yr * yr
            rot = yr * gain_table[:, :LANES] + ys * gain_table[:, LANES:]
            for par in range(2):
                sq = nope[par] * nope[par] + jnp.where(lo if par == 0 else ~lo, sq_r, 0.0)
                r = lax.rsqrt(jnp.sum(sq, axis=-1, keepdims=True) / QK_DIM + EPS)
                hd = 2 * pair + par
                o_ref[0, hd, rows, 0:LANES] = (nope[par] * r * gn_ref[0]).astype(BF16)
                o_ref[0, hd, rows, LANES:QK_STORE] = jnp.where(lo if par == 0 else ~lo, rot * r, 0.0).astype(BF16)


def _q_proj(h, w_in_t, lora_g, wuq, gn, g2, cs, layer, batch, seq, tm):
    nt = seq // tm
    vec = lambda width: pl.BlockSpec((1, 1, width), lambda i: (layer, 0, 0))
    return pl.pallas_call(
        _q_proj_kernel,
        grid=(batch * nt,),
        in_specs=[
            pl.BlockSpec((tm, D_MODEL), lambda i: (i, 0)),
            pl.BlockSpec((None, Q_LORA, D_MODEL), lambda i: (layer, OFF_Q // Q_LORA, 0)),
            vec(Q_LORA),
            pl.BlockSpec((None, N_HEADS // 2, Q_LORA, 4 * LANES), lambda i: (layer, 0, 0, 0)),
            vec(LANES), vec(2 * LANES),
            pl.BlockSpec((tm, 2 * LANES), lambda i: (i % nt, 0)),
        ],
        out_specs=pl.BlockSpec((1, N_HEADS, tm, QK_STORE), lambda i: (i // nt, 0, i % nt, 0)),
        out_shape=jax.ShapeDtypeStruct((batch, N_HEADS, seq, QK_STORE), BF16),
        compiler_params=_params("arbitrary"),
        name="q_proj",
    )(h, w_in_t, lora_g.reshape(DEPTH, 1, Q_LORA), wuq, gn, g2, cs)


def _kv_proj_kernel(h_ref, wkv_ref, wkr_ref, lg_ref, wukv_ref, gn_ref, g2_ref, cs_ref, k_ref, v_ref):
    wkv = wkv_ref[...].astype(BF16)
    q4 = QK_ROPE // 4
    wkr = jnp.concatenate([wkr_ref[...]] + [wkr_ref[s * q4:(s + 1) * q4, :] for s in (1, 0, 3, 2)], axis=0)
    wkr = wkr.astype(BF16)
    lo = _lo_lanes()
    for rows in _row_blocks(h_ref.shape[0], PROJ_ROWS):
        h = h_ref[rows, :]
        zc = _dot_nt(h, wkv)
        kr = _dot_nt(h, wkr)
        ckv = zc * lax.rsqrt(jnp.mean(zc * zc, axis=-1, keepdims=True) + EPS) * lg_ref[0]
        ckv = ckv.astype(BF16)
        ss_rope = jnp.sum(jnp.where(lo, kr * kr, 0.0), axis=-1, keepdims=True)
        u = kr * (g2_ref[0] * cs_ref[rows, :])
        rot = u + pltpu.roll(u, LANES // 2, 1)
        rope = (jnp.where(lo, rot, 0.0), jnp.where(lo, 0.0, rot))
        for hd in range(N_HEADS):
            kv = _dot(ckv, wukv_ref[:, hd * 2 * LANES:(hd + 1) * 2 * LANES])
            kn, v = kv[:, :QK_NOPE], kv[:, QK_NOPE:]
            ss = jnp.sum(kn * kn, axis=-1, keepdims=True) + ss_rope
            r = lax.rsqrt(ss / QK_DIM + EPS)
            k_ref[0, hd, rows, 0:LANES] = (kn * r * gn_ref[0]).astype(BF16)
            k_ref[0, hd, rows, LANES:QK_STORE] = (rope[hd % 2] * r).astype(BF16)
            v_ref[0, hd, rows, :] = v.astype(BF16)


def _kv_proj(h, w_in_t, lora_g, wukv, gn, g2, cs, layer, batch, seq, tm):
    nt = seq // tm
    vec = lambda width: pl.BlockSpec((1, 1, width), lambda i: (layer, 0, 0))
    return pl.pallas_call(
        _kv_proj_kernel,
        grid=(batch * nt,),
        in_specs=[
            pl.BlockSpec((tm, D_MODEL), lambda i: (i, 0)),
            pl.BlockSpec((None, KV_LORA, D_MODEL), lambda i: (layer, OFF_KV // KV_LORA, 0)),
            pl.BlockSpec((None, QK_ROPE, D_MODEL), lambda i: (layer, OFF_KROPE // QK_ROPE, 0)),
            vec(KV_LORA),
            pl.BlockSpec((None, KV_LORA, N_HEADS * 2 * LANES), lambda i: (layer, 0, 0)),
            vec(LANES), vec(LANES),
            pl.BlockSpec((tm, LANES), lambda i: (i % nt, 0)),
        ],
        out_specs=[
            pl.BlockSpec((1, N_HEADS, tm, QK_STORE), lambda i: (i // nt, 0, i % nt, 0)),
            pl.BlockSpec((1, N_HEADS, tm, V_DIM), lambda i: (i // nt, 0, i % nt, 0)),
        ],
        out_shape=[
            jax.ShapeDtypeStruct((batch, N_HEADS, seq, QK_STORE), BF16),
            jax.ShapeDtypeStruct((batch, N_HEADS, seq, V_DIM), BF16),
        ],
        compiler_params=_params("arbitrary"),
        name="kv_proj",
    )(h, w_in_t, w_in_t, lora_g.reshape(DEPTH, 1, KV_LORA), wukv, gn, g2, cs)


def _attn_kernel(q_ref, *refs, tq):
    o_ref = refs[-1]
    kv_refs = refs[:-1]
    scale = ATTN_SCALE * math.log2(math.e)
    steps = []
    for s in range(0, len(kv_refs), 2):
        n_keys = kv_refs[s].shape[2]
        tk = min(KEY_CHUNK, n_keys)
        steps += [(kv_refs[s], kv_refs[s + 1], j, tk) for j in range(0, n_keys, tk)]
    for hd in range(q_ref.shape[1]):
        for r in range(0, q_ref.shape[2], tq):
            q = q_ref[0, hd, r:r + tq, :]
            m = acc = None
            for k_ref, v_ref, j, n in steps:
                sc = _dot_nt(q, k_ref[0, hd, j:j + n, :])
                m_new = jnp.max(sc, axis=-1, keepdims=True)
                if m is not None:
                    m_new = jnp.maximum(m, m_new)
                p = jnp.exp2((sc - m_new) * scale).astype(BF16)
                pv = _dot(p, jnp.concatenate([v_ref[0, hd, j:j + n, :], jnp.ones((n, LANES), BF16)], axis=1))
                acc = pv if m is None else jnp.exp2((m - m_new) * scale) * acc + pv
                m = m_new
            o_ref[0, r:r + tq, hd * V_DIM:(hd + 1) * V_DIM] = (acc[:, :V_DIM] / acc[:, V_DIM:]).astype(BF16)


def _attention(q, kvs, tq, heads_per_step):
    batch, _, seq, _ = q.shape
    hps = heads_per_step
    head_spec = lambda rows, width: pl.BlockSpec((1, hps, rows, width), lambda b, hg: (b, hg, 0, 0))
    in_specs = [head_spec(seq, QK_STORE)]
    args = [q]
    for k, v in kvs:
        in_specs += [head_spec(k.shape[2], QK_STORE), head_spec(v.shape[2], V_DIM)]
        args += [k, v]
    return pl.pallas_call(
        functools.partial(_attn_kernel, tq=tq),
        grid=(batch, N_HEADS // hps),
        in_specs=in_specs,
        out_specs=pl.BlockSpec((1, seq, hps * V_DIM), lambda b, hg: (b, 0, hg)),
        out_shape=jax.ShapeDtypeStruct((batch, seq, N_HEADS * V_DIM), BF16),
        compiler_params=_params("arbitrary", "arbitrary"),
        name="attention",
    )(*args)


def _merge_kernel(h_ref, pa_ref, pb_ref, oc_ref, wg0_ref, wg1_ref, wg2_ref, wa_ref, wb_ref, wc_ref, o_ref):
    branches = ((wg0_ref, pa_ref, wa_ref), (wg1_ref, pb_ref, wb_ref), (wg2_ref, oc_ref, wc_ref))
    weights = [w_ref[...].astype(BF16) for _, _, w_ref in branches]
    for rows in _row_blocks(o_ref.shape[0]):
        merged = None
        for (wg_ref, y_ref, _), w in zip(branches, weights):
            term = jax.nn.sigmoid(_dot_nt(h_ref[rows, :], wg_ref[...])) * _dot(y_ref[rows, :], w)
            merged = term if merged is None else merged + term
        o_ref[rows, :] = merged.astype(BF16)


def _merge(h, pa, pb, oc, wg_t, wba, wbb, wbc, layer, tm, tn):
    rows = h.shape[0]
    nb = D_MODEL // tn
    row_spec = lambda width: pl.BlockSpec((tm, width), lambda i, j: (i, 0))
    gate_spec = lambda k: pl.BlockSpec((None, tn, D_MODEL), lambda i, j: (layer, k * nb + j, 0))
    col_spec = lambda depth: pl.BlockSpec((None, depth, tn), lambda i, j: (layer, 0, j))
    return pl.pallas_call(
        _merge_kernel,
        grid=(rows // tm, nb),
        in_specs=[
            row_spec(D_MODEL), row_spec(POOL_WIDTH), row_spec(CONV_WIDTH), row_spec(N_HEADS * V_DIM),
            gate_spec(0), gate_spec(1), gate_spec(2),
            col_spec(POOL_WIDTH), col_spec(CONV_WIDTH), col_spec(N_HEADS * V_DIM),
        ],
        out_specs=pl.BlockSpec((tm, tn), lambda i, j: (i, j)),
        out_shape=jax.ShapeDtypeStruct((rows, D_MODEL), BF16),
        compiler_params=_params("arbitrary", "arbitrary"),
        name="merge",
    )(h, pa, pb, oc, wg_t, wg_t, wg_t, wba, wbb, wbc)


def _matmul_residual_kernel(a_ref, w_ref, res_ref, gate_ref, o_ref):
    w = w_ref[...].astype(BF16)
    for rows in _row_blocks(o_ref.shape[0]):
        o_ref[rows, :] = res_ref[rows, :] + gate_ref[0] * _dot(a_ref[rows, :], w)


def _matmul_residual(a, w, res, mods, seg_of_tile, layer, which_gate, tm, tn):
    rows, depth = a.shape
    return pl.pallas_call(
        _matmul_residual_kernel,
        grid=(rows // tm, D_MODEL // tn),
        in_specs=[
            pl.BlockSpec((tm, depth), lambda i, j: (i, 0)),
            pl.BlockSpec((None, depth, tn), lambda i, j: (layer, 0, j)),
            pl.BlockSpec((tm, tn), lambda i, j: (i, j)),
            pl.BlockSpec((1, 1, tn), lambda i, j: (_mod_row(layer, seg_of_tile(i), which_gate), 0, j)),
        ],
        out_specs=pl.BlockSpec((tm, tn), lambda i, j: (i, j)),
        out_shape=jax.ShapeDtypeStruct((rows, D_MODEL), F32),
        compiler_params=_params("arbitrary", "arbitrary"),
        name="matmul_residual",
    )(a, w, res, mods)


def _matmul_residual_norm_kernel(a_ref, w_ref, res_ref, gate_ref, g_ref, sh_ref, sc_ref, x_ref, h_ref):
    a = a_ref[...]
    for n in range(0, D_MODEL, MXU_ROWS):
        cols = slice(n, n + MXU_ROWS)
        x_ref[:, cols] = res_ref[:, cols] + gate_ref[0, :, cols] * _dot(a, w_ref[:, cols])
    x = x_ref[...]
    y = x * lax.rsqrt(jnp.mean(x * x, axis=-1, keepdims=True) + EPS) * g_ref[0]
    h_ref[...] = (y * (1 + sc_ref[0]) + sh_ref[0]).astype(BF16)


def _matmul_residual_norm(a, w, res, mods, g, seg_of_tile, layer, which_gate, which_shift, tm):
    rows, depth = a.shape
    mod = lambda which: pl.BlockSpec((1, 1, D_MODEL), lambda i: (_mod_row(layer, seg_of_tile(i), which), 0, 0))
    row_spec = lambda width: pl.BlockSpec((tm, width), lambda i: (i, 0))
    return pl.pallas_call(
        _matmul_residual_norm_kernel,
        grid=(rows // tm,),
        in_specs=[
            row_spec(depth),
            pl.BlockSpec((None, depth, D_MODEL), lambda i: (layer, 0, 0)),
            row_spec(D_MODEL),
            mod(which_gate),
            pl.BlockSpec((1, 1, D_MODEL), lambda i: (layer, 0, 0)),
            mod(which_shift), mod(which_shift + 1),
        ],
        out_specs=[row_spec(D_MODEL), row_spec(D_MODEL)],
        out_shape=[jax.ShapeDtypeStruct((rows, D_MODEL), F32), jax.ShapeDtypeStruct((rows, D_MODEL), BF16)],
        compiler_params=_params("arbitrary"),
        name="matmul_residual_norm",
    )(a, w, res, mods, g.reshape(DEPTH, 1, D_MODEL), mods, mods)


def _ffn_up_kernel(h_ref, wu_ref, wv_ref, cw_ref, o_ref, *, seq, sub):
    for n in range(0, o_ref.shape[1], sub):
        cols = slice(n, n + sub)
        cu = _conv3(_dot_rows(h_ref, wu_ref[:, cols].astype(BF16)), cw_ref.at[:, cols], seq)
        gate = _dot_rows(h_ref, wv_ref[:, cols].astype(BF16))
        o_ref[:, cols] = (cu * jax.nn.sigmoid(cu) * gate).astype(BF16)


def _ffn_up(h, w_up, conv, layer, seq, tm, tn):
    rows = h.shape[0]
    nb = D_FF // tn
    return pl.pallas_call(
        functools.partial(_ffn_up_kernel, seq=seq, sub=min(tn, 256)),
        grid=(rows // tm, nb),
        in_specs=[
            pl.BlockSpec((tm, D_MODEL), lambda b, f: (b, 0)),
            pl.BlockSpec((None, D_MODEL, tn), lambda b, f: (layer, 0, f)),
            pl.BlockSpec((None, D_MODEL, tn), lambda b, f: (layer, 0, nb + f)),
            pl.BlockSpec((None, 3, tn), lambda b, f: (layer, 0, f)),
        ],
        out_specs=pl.BlockSpec((tm, tn), lambda b, f: (b, f)),
        out_shape=jax.ShapeDtypeStruct((rows, D_FF), BF16),
        compiler_params=_params("arbitrary", "arbitrary"),
        name="ffn_up",
    )(h, w_up, w_up, conv)


_ROPE_SWAP = np.concatenate([np.arange(16, 32), np.arange(0, 16), np.arange(48, 64), np.arange(32, 48)])


def _rope_cos_sin(length):
    rows = length // GRID_W
    row = np.repeat(np.arange(rows, dtype=np.int32), GRID_W).astype(np.float32)
    col = np.tile(np.arange(GRID_W, dtype=np.int32), rows).astype(np.float32)
    inv = np.float32(ROPE_THETA) ** (-np.arange(0, ROPE_AXIS, 2, dtype=np.float32) / np.float32(ROPE_AXIS))
    ang_r = row[:, None] * inv[None]
    ang_c = col[:, None] * inv[None]
    cr, sr, cc, sc = np.cos(ang_r), np.sin(ang_r), np.cos(ang_c), np.sin(ang_c)
    return np.concatenate([cr, cr, cc, cc], axis=-1), np.concatenate([-sr, sr, -sc, sc], axis=-1)


def _identity_cos_sin(length):
    return np.ones((length, QK_ROPE), np.float32), np.zeros((length, QK_ROPE), np.float32)


def _rope_tables(cos, sin):
    return (jnp.asarray(np.concatenate([cos, cos, sin, sin], axis=-1), F32),
            jnp.asarray(np.concatenate([cos, sin], axis=-1), F32))


def _prepared_weights(w_in_t, w_uq, w_ukv, w_out, w_ffn_down, q_head_g, k_head_g):
    uq = jnp.transpose(w_uq, (0, 2, 1, 3)).reshape(DEPTH, N_HEADS // 2, 2, Q_LORA, QK_DIM)
    even, odd = uq[:, :, 0], uq[:, :, 1]
    swapped = lambda w: w[..., QK_NOPE:][..., _ROPE_SWAP]
    q_rope, k_rope_g = q_head_g[:, QK_NOPE:], k_head_g[:, QK_NOPE:]
    q_swap, k_swap = q_rope[:, _ROPE_SWAP], k_rope_g[:, _ROPE_SWAP]
    return dict(
        wg_t=w_in_t[:, OFF_G:, :].astype(BF16),
        w_down=w_ffn_down.astype(BF16),
        w_out=w_out.astype(BF16),
        wuq=jnp.concatenate([even[..., :QK_NOPE], odd[..., :QK_NOPE], even[..., QK_NOPE:], odd[..., QK_NOPE:],
                             swapped(even), swapped(odd)], axis=-1).astype(BF16),
        wukv=w_ukv.reshape(DEPTH, KV_LORA, N_HEADS * (QK_NOPE + V_DIM)).astype(BF16),
        q_gn=q_head_g[:, None, :QK_NOPE], k_gn=k_head_g[:, None, :QK_NOPE],
        q_g2=jnp.concatenate([q_rope, q_rope, q_swap, q_swap], axis=-1)[:, None],
        k_g2=jnp.concatenate([k_rope_g, k_swap], axis=-1)[:, None],
    )


def _tiles(batch, seq):
    rows = batch * seq
    return dict(
        tm=min(rows, 1024),
        seq_tm=seq if seq >= 1024 else rows,
        full_row_tm=min(rows, 512),
        proj_tm=min(seq, 512),
        q_tm=min(seq, 512),
        tq=min(seq, 512),
        attn_heads=max(1, min(N_HEADS, 4096 // seq)),
        narrow_tn=256,
        wide_tn=512,
    )


def kernel(x, c, ctx, c_ctx, norm1_g, norm2_g, w_mod, b_mod, w_in, pool_w, pool_scale, conv_w, q_lora_g, w_uq,
           kv_lora_g, w_ukv, q_head_g, k_head_g, w_branch_a, w_branch_b, w_branch_c, w_out, w_ffn_up, ffn_conv,
           w_ffn_down):
    batch, seq, _ = x.shape
    ctx_len = ctx.shape[1]
    assert batch + 1 <= MOD_ROWS
    cv = jnp.concatenate([c, c_ctx[None], jnp.zeros((MOD_ROWS - batch - 1, D_MODEL), F32)], axis=0)
    mods = _modulation(cv, w_mod, b_mod)
    w_in_t = jnp.swapaxes(w_in, 1, 2)
    pw = _prepared_weights(w_in_t, w_uq, w_ukv, w_out, w_ffn_down, q_head_g, k_head_g)

    def kv_proj(h, layer, cs, n_tok, tiles):
        return _kv_proj(h, w_in_t, kv_lora_g, pw['wukv'], pw['k_gn'], pw['k_g2'], cs[1], layer, batch, n_tok,
                        tiles['proj_tm'])

    def stream_layer(xs, layer, seg_of_tile, cs, n_tok, ctx_kv, tiles):
        tm, seq_tm = tiles['tm'], tiles['seq_tm']
        h = _norm_mod(xs, norm1_g, mods, seg_of_tile(tm), layer, 0, tm)
        pa = _branch_a(h, w_in_t, pool_w, pool_scale, layer, n_tok)
        pb = _branch_b(h, w_in_t, conv_w, layer, n_tok, seq_tm, tiles['narrow_tn'])
        q = _q_proj(h, w_in_t, q_lora_g, pw['wuq'], pw['q_gn'], pw['q_g2'], cs[0], layer, batch, n_tok,
                    tiles['q_tm'])
        kv = kv_proj(h, layer, cs, n_tok, tiles)
        kvs = [kv] if ctx_kv is None else [kv, ctx_kv]
        oc = _attention(q, kvs, tiles['tq'], tiles['attn_heads']).reshape(batch * n_tok, N_HEADS * V_DIM)
        merged = _merge(h, pa, pb, oc, pw['wg_t'], w_branch_a, w_branch_b, w_branch_c, layer, tm,
                        tiles['narrow_tn'])
        full_tm = tiles['full_row_tm']
        xs, h2 = _matmul_residual_norm(merged, pw['w_out'], xs, mods, norm2_g, seg_of_tile(full_tm), layer, 2, 3,
                                       full_tm)
        hidden = _ffn_up(h2, w_ffn_up, ffn_conv, layer, n_tok, seq_tm, tiles['wide_tn'])
        xs = _matmul_residual(hidden, pw['w_down'], xs, mods, seg_of_tile(tm), layer, 5, tm, tiles['wide_tn'])
        return xs, kv

    cs_x = _rope_tables(*_rope_cos_sin(seq))
    cs_c = _rope_tables(*_identity_cos_sin(ctx_len))
    x_tiles = _tiles(batch, seq)
    c_tiles = _tiles(batch, ctx_len)
    x_seg = lambda tm: (lambda i: i // (seq // tm))
    c_seg = lambda tm: (lambda i: batch)

    xs = x.reshape(batch * seq, D_MODEL)
    cx = ctx.reshape(batch * ctx_len, D_MODEL)
    for layer in range(DEPTH):
        if layer == DEPTH - 1:
            hc = _norm_mod(cx, norm1_g, mods, c_seg(c_tiles['tm']), layer, 0, c_tiles['tm'])
            ctx_kv = kv_proj(hc, layer, cs_c, ctx_len, c_tiles)
        else:
            cx, ctx_kv = stream_layer(cx, layer, c_seg, cs_c, ctx_len, None, c_tiles)
        xs, _ = stream_layer(xs, layer, x_seg, cs_x, seq, tuple(ctx_kv), x_tiles)
    return xs.reshape(batch, seq, D_MODEL)
```

```python
import functools
import math

import jax
import jax.numpy as jnp
import numpy as np
from jax import lax
from jax.experimental import pallas as pl
from jax.experimental.pallas import tpu as pltpu

D_MODEL = 2048
DEPTH = 2
GRID_W = 64
POOL_WINDOWS = (2, 4, 8, 16)
POOL_GROUP = D_MODEL // 8
POOL_WIDTH = POOL_GROUP * len(POOL_WINDOWS)
CONV_WIDTH = D_MODEL // 2
N_HEADS = 16
QK_NOPE = 128
QK_ROPE = 64
ROPE_AXIS = QK_ROPE // 2
V_DIM = 128
QK_DIM = QK_NOPE + QK_ROPE
Q_LORA = 512
KV_LORA = 512
ROPE_THETA = 10000.0
ATTN_SCALE = QK_DIM ** -0.5
D_FF = 5632
N_BRANCH = 3
EPS = 1e-6
OFF_A = 0
OFF_B = OFF_A + POOL_WIDTH
OFF_Q = OFF_B + 3 * CONV_WIDTH
OFF_KV = OFF_Q + Q_LORA
OFF_KROPE = OFF_KV + KV_LORA
OFF_G = OFF_KROPE + QK_ROPE

LANES = 128
QK_STORE = 2 * LANES
MXU_ROWS = 512
PROJ_ROWS = 512
POOL_PAD = 16
MOD_ROWS = 8
N_MOD = 6
VMEM_LIMIT = 56 * 1024 * 1024
KEY_CHUNK = 512

BF16 = jnp.bfloat16
F32 = jnp.float32


def _dot(a, b):
    return jnp.dot(a, b, preferred_element_type=F32)


def _dot_nt(a, b_t):
    return lax.dot_general(a, b_t, (((1,), (1,)), ((), ())), preferred_element_type=F32)


def _dot_rows(a_ref, w, dot=_dot):
    rows = a_ref.shape[0]
    if rows <= MXU_ROWS:
        return dot(a_ref[...], w)
    return jnp.concatenate([dot(a_ref[r:r + MXU_ROWS, :], w) for r in range(0, rows, MXU_ROWS)], axis=0)


def _row_blocks(rows, block=MXU_ROWS):
    return [slice(r, min(r + block, rows)) for r in range(0, rows, block)]


def _cast_rider(src, layer, n_steps, step_index):
    rows, width = src.shape[1:]
    slab = rows // n_steps
    assert slab * n_steps == rows and slab % 16 == 0
    in_spec = pl.BlockSpec((None, slab, width), lambda *g: (layer, step_index(*g), 0))
    out_spec = pl.BlockSpec((slab, width), lambda *g: (step_index(*g), 0))
    return in_spec, out_spec, jax.ShapeDtypeStruct((rows, width), BF16)


def _in_proj(h_ref, wt_ref):
    return _dot_rows(h_ref, wt_ref[...].astype(BF16), _dot_nt)


def _params(*sem):
    return pltpu.CompilerParams(dimension_semantics=sem, vmem_limit_bytes=VMEM_LIMIT)


def _mod_row(layer, seg, which):
    return (layer * MOD_ROWS + seg) * N_MOD + which


def _mod_kernel(cv_ref, w_ref, b_ref, o_ref):
    a = cv_ref[...]
    a = a * jax.nn.sigmoid(a)
    o_ref[...] = _dot(a.astype(BF16), w_ref[...].astype(BF16)) + b_ref[...]


def _modulation(cv, w_mod, b_mod):
    tn = 1024
    n_out = N_MOD * D_MODEL
    out = pl.pallas_call(
        _mod_kernel,
        grid=(DEPTH, n_out // tn),
        in_specs=[
            pl.BlockSpec((MOD_ROWS, D_MODEL), lambda l, j: (0, 0)),
            pl.BlockSpec((None, D_MODEL, tn), lambda l, j: (l, 0, j)),
            pl.BlockSpec((None, 1, tn), lambda l, j: (l, 0, j)),
        ],
        out_specs=pl.BlockSpec((None, MOD_ROWS, tn), lambda l, j: (l, 0, j)),
        out_shape=jax.ShapeDtypeStruct((DEPTH, MOD_ROWS, n_out), F32),
        compiler_params=_params("arbitrary", "arbitrary"),
        name="modulation",
    )(cv, w_mod, b_mod.reshape(DEPTH, 1, n_out))
    return out.reshape(DEPTH * MOD_ROWS * N_MOD, 1, D_MODEL)


def _norm_mod_kernel(x_ref, g_ref, sh_ref, sc_ref, o_ref):
    x = x_ref[...]
    y = x * lax.rsqrt(jnp.mean(x * x, axis=-1, keepdims=True) + EPS)
    y = y * g_ref[0]
    o_ref[...] = (y * (1 + sc_ref[0]) + sh_ref[0]).astype(BF16)


def _norm_mod(x, g, mods, seg_of_tile, layer, which_shift, tm):
    rows = x.shape[0]
    return pl.pallas_call(
        _norm_mod_kernel,
        grid=(rows // tm,),
        in_specs=[
            pl.BlockSpec((tm, D_MODEL), lambda i: (i, 0)),
            pl.BlockSpec((1, 1, D_MODEL), lambda i: (layer, 0, 0)),
            pl.BlockSpec((1, 1, D_MODEL), lambda i: (_mod_row(layer, seg_of_tile(i), which_shift), 0, 0)),
            pl.BlockSpec((1, 1, D_MODEL), lambda i: (_mod_row(layer, seg_of_tile(i), which_shift + 1), 0, 0)),
        ],
        out_specs=pl.BlockSpec((tm, D_MODEL), lambda i: (i, 0)),
        out_shape=jax.ShapeDtypeStruct((rows, D_MODEL), BF16),
        compiler_params=_params("arbitrary"),
        name="norm_mod",
    )(x, g.reshape(DEPTH, 1, D_MODEL), mods, mods)


def _branch_a_kernel(h_ref, *refs, seq):
    n_win = len(POOL_WINDOWS)
    wa_refs = refs[:n_win]
    pw_ref, ps_ref, o_ref, ext_ref = refs[n_win:]
    n_ext = seq + 2 * POOL_PAD
    zeros = jnp.zeros((POOL_PAD, POOL_GROUP), F32)
    ext_ref[0:POOL_PAD, :] = zeros
    ext_ref[seq + POOL_PAD:n_ext, :] = zeros
    t = lax.broadcasted_iota(jnp.int32, (seq, 1), 0)
    for g, w in enumerate(POOL_WINDOWS):
        cols = slice(g * POOL_GROUP, (g + 1) * POOL_GROUP)
        a = _in_proj(h_ref, wa_refs[g])
        ext_ref[POOL_PAD:seq + POOL_PAD, :] = a
        e = ext_ref[...]
        c = e + pltpu.roll(e, 1, 0)
        half = 1
        while 2 * half < w:
            c = pltpu.roll(c, half, 0) + pltpu.roll(c, n_ext - half, 0)
            half *= 2
        cnt = (jnp.minimum(t + w // 2, seq) - jnp.maximum(t - w // 2, 0)).astype(F32)
        pooled = c[POOL_PAD:seq + POOL_PAD, :] / cnt - a
        mixed = _dot(pooled.astype(BF16), pw_ref[g].astype(BF16)) * ps_ref[0, :, cols]
        o_ref[:, cols] = mixed.astype(BF16)


def _branch_a(h, w_in_t, pool_w, pool_scale, layer, seq):
    rows = h.shape[0]
    n_win = len(POOL_WINDOWS)
    group_rows = lambda g: pl.BlockSpec((None, POOL_GROUP, D_MODEL), lambda b: (layer, OFF_A // POOL_GROUP + g, 0))
    return pl.pallas_call(
        functools.partial(_branch_a_kernel, seq=seq),
        grid=(rows // seq,),
        in_specs=[pl.BlockSpec((seq, D_MODEL), lambda b: (b, 0))] + [group_rows(g) for g in range(n_win)] + [
            pl.BlockSpec((None, n_win, POOL_GROUP, POOL_GROUP), lambda b: (layer, 0, 0, 0)),
            pl.BlockSpec((1, 1, POOL_WIDTH), lambda b: (layer, 0, 0)),
        ],
        out_specs=pl.BlockSpec((seq, POOL_WIDTH), lambda b: (b, 0)),
        out_shape=jax.ShapeDtypeStruct((rows, POOL_WIDTH), BF16),
        scratch_shapes=[pltpu.VMEM((seq + 2 * POOL_PAD, POOL_GROUP), F32)],
        compiler_params=_params("arbitrary"),
        name="branch_a",
    )(h, *([w_in_t] * n_win), pool_w, pool_scale.reshape(DEPTH, 1, POOL_WIDTH))


def _conv3(u, cw_ref, seq):
    rows = u.shape[0]
    t = lax.broadcasted_iota(jnp.int32, (rows, 1), 0) % seq
    prev = jnp.where(t == 0, 0.0, pltpu.roll(u, 1, 0))
    nxt = jnp.where(t == seq - 1, 0.0, pltpu.roll(u, rows - 1, 0))
    return prev * cw_ref[0:1, :] + u * cw_ref[1:2, :] + nxt * cw_ref[2:3, :]


def _branch_b_kernel(h_ref, wgb_ref, wgc_ref, wx_ref, cw_ref, o_ref, *, seq):
    u = _in_proj(h_ref, wgc_ref) * _in_proj(h_ref, wx_ref)
    o_ref[...] = (_in_proj(h_ref, wgb_ref) * _conv3(u, cw_ref, seq)).astype(BF16)


def _branch_b(h, w_in_t, conv_w, layer, seq, tm, tn):
    rows = h.shape[0]
    nb = CONV_WIDTH // tn
    first = OFF_B // tn
    w_spec = lambda k: pl.BlockSpec((None, tn, D_MODEL), lambda b, c: (layer, first + k * nb + c, 0))
    return pl.pallas_call(
        functools.partial(_branch_b_kernel, seq=seq),
        grid=(rows // tm, nb),
        in_specs=[
            pl.BlockSpec((tm, D_MODEL), lambda b, c: (b, 0)),
            w_spec(0), w_spec(1), w_spec(2),
            pl.BlockSpec((None, 3, tn), lambda b, c: (layer, 0, c)),
        ],
        out_specs=pl.BlockSpec((tm, tn), lambda b, c: (b, c)),
        out_shape=jax.ShapeDtypeStruct((rows, CONV_WIDTH), BF16),
        compiler_params=_params("arbitrary", "arbitrary"),
        name="branch_b",
    )(h, w_in_t, w_in_t, w_in_t, conv_w)


def _lo_lanes():
    return lax.broadcasted_iota(jnp.int32, (1, LANES), 1) < QK_ROPE


def _q_proj_kernel(h_ref, wq_ref, lg_ref, wuq_ref, gn_ref, g2_ref, cs_ref, o_ref):
    wq = wq_ref[...].astype(BF16)
    lo = _lo_lanes()
    for rows in _row_blocks(h_ref.shape[0], PROJ_ROWS):
        zq = _dot_nt(h_ref[rows, :], wq)
        cq = zq * lax.rsqrt(jnp.mean(zq * zq, axis=-1, keepdims=True) + EPS) * lg_ref[0]
        cq = cq.astype(BF16)
        gain_table = g2_ref[0] * cs_ref[rows, :]
        for pair in range(N_HEADS // 2):
            y = _dot(cq, wuq_ref[pair])
            nope = (y[:, 0:LANES], y[:, LANES:2 * LANES])
            yr, ys = y[:, 2 * LANES:3 * LANES], y[:, 3 * LANES:]
            sq_r = yr * yr
            rot = yr * gain_table[:, :LANES] + ys * gain_table[:, LANES:]
            for par in range(2):
                sq = nope[par] * nope[par] + jnp.where(lo if par == 0 else ~lo, sq_r, 0.0)
                r = lax.rsqrt(jnp.sum(sq, axis=-1, keepdims=True) / QK_DIM + EPS)
                hd = 2 * pair + par
                o_ref[0, hd, rows, 0:LANES] = (nope[par] * r * gn_ref[0]).astype(BF16)
                o_ref[0, hd, rows, LANES:QK_STORE] = jnp.where(lo if par == 0 else ~lo, rot * r, 0.0).astype(BF16)


def _q_proj(h, w_in_t, lora_g, wuq, gn, g2, cs, layer, batch, seq, tm):
    nt = seq // tm
    vec = lambda width: pl.BlockSpec((1, 1, width), lambda i: (layer, 0, 0))
    return pl.pallas_call(
        _q_proj_kernel,
        grid=(batch * nt,),
        in_specs=[
            pl.BlockSpec((tm, D_MODEL), lambda i: (i, 0)),
            pl.BlockSpec((None, Q_LORA, D_MODEL), lambda i: (layer, OFF_Q // Q_LORA, 0)),
            vec(Q_LORA),
            pl.BlockSpec((None, N_HEADS // 2, Q_LORA, 4 * LANES), lambda i: (layer, 0, 0, 0)),
            vec(LANES), vec(2 * LANES),
            pl.BlockSpec((tm, 2 * LANES), lambda i: (i % nt, 0)),
        ],
        out_specs=pl.BlockSpec((1, N_HEADS, tm, QK_STORE), lambda i: (i // nt, 0, i % nt, 0)),
        out_shape=jax.ShapeDtypeStruct((batch, N_HEADS, seq, QK_STORE), BF16),
        compiler_params=_params("arbitrary"),
        name="q_proj",
    )(h, w_in_t, lora_g.reshape(DEPTH, 1, Q_LORA), wuq, gn, g2, cs)


def _kv_proj_kernel(h_ref, wkv_ref, wkr_ref, lg_ref, wukv_ref, gn_ref, g2_ref, cs_ref, *rest):
    if len(rest) == 4:
        cast_src_ref, k_ref, v_ref, cast_dst_ref = rest
        cast_dst_ref[...] = cast_src_ref[...].astype(BF16)
    else:
        k_ref, v_ref = rest
    wkv = wkv_ref[...].astype(BF16)
    q4 = QK_ROPE // 4
    wkr = jnp.concatenate([wkr_ref[...]] + [wkr_ref[s * q4:(s + 1) * q4, :] for s in (1, 0, 3, 2)], axis=0)
    wkr = wkr.astype(BF16)
    lo = _lo_lanes()
    for rows in _row_blocks(h_ref.shape[0], PROJ_ROWS):
        h = h_ref[rows, :]
        zc = _dot_nt(h, wkv)
        kr = _dot_nt(h, wkr)
        ckv = zc * lax.rsqrt(jnp.mean(zc * zc, axis=-1, keepdims=True) + EPS) * lg_ref[0]
        ckv = ckv.astype(BF16)
        ss_rope = jnp.sum(jnp.where(lo, kr * kr, 0.0), axis=-1, keepdims=True)
        u = kr * (g2_ref[0] * cs_ref[rows, :])
        rot = u + pltpu.roll(u, LANES // 2, 1)
        rope = (jnp.where(lo, rot, 0.0), jnp.where(lo, 0.0, rot))
        for hd in range(N_HEADS):
            kv = _dot(ckv, wukv_ref[:, hd * 2 * LANES:(hd + 1) * 2 * LANES])
            kn, v = kv[:, :QK_NOPE], kv[:, QK_NOPE:]
            ss = jnp.sum(kn * kn, axis=-1, keepdims=True) + ss_rope
            r = lax.rsqrt(ss / QK_DIM + EPS)
            k_ref[0, hd, rows, 0:LANES] = (kn * r * gn_ref[0]).astype(BF16)
            k_ref[0, hd, rows, LANES:QK_STORE] = (rope[hd % 2] * r).astype(BF16)
            v_ref[0, hd, rows, :] = v.astype(BF16)


def _kv_proj(h, w_in_t, lora_g, wukv, gn, g2, cs, layer, batch, seq, tm, cast_src=None):
    nt = seq // tm
    vec = lambda width: pl.BlockSpec((1, 1, width), lambda i: (layer, 0, 0))
    in_specs = [
        pl.BlockSpec((tm, D_MODEL), lambda i: (i, 0)),
        pl.BlockSpec((None, KV_LORA, D_MODEL), lambda i: (layer, OFF_KV // KV_LORA, 0)),
        pl.BlockSpec((None, QK_ROPE, D_MODEL), lambda i: (layer, OFF_KROPE // QK_ROPE, 0)),
        vec(KV_LORA),
        pl.BlockSpec((None, KV_LORA, N_HEADS * 2 * LANES), lambda i: (layer, 0, 0)),
        vec(LANES), vec(LANES),
        pl.BlockSpec((tm, LANES), lambda i: (i % nt, 0)),
    ]
    out_specs = [
        pl.BlockSpec((1, N_HEADS, tm, QK_STORE), lambda i: (i // nt, 0, i % nt, 0)),
        pl.BlockSpec((1, N_HEADS, tm, V_DIM), lambda i: (i // nt, 0, i % nt, 0)),
    ]
    out_shape = [
        jax.ShapeDtypeStruct((batch, N_HEADS, seq, QK_STORE), BF16),
        jax.ShapeDtypeStruct((batch, N_HEADS, seq, V_DIM), BF16),
    ]
    args = [h, w_in_t, w_in_t, lora_g.reshape(DEPTH, 1, KV_LORA), wukv, gn, g2, cs]
    if cast_src is not None:
        src_spec, dst_spec, dst_shape = _cast_rider(cast_src, layer, batch * nt, lambda i: i)
        in_specs.append(src_spec)
        out_specs.append(dst_spec)
        out_shape.append(dst_shape)
        args.append(cast_src)
    outs = pl.pallas_call(
        _kv_proj_kernel,
        grid=(batch * nt,),
        in_specs=in_specs,
        out_specs=out_specs,
        out_shape=out_shape,
        compiler_params=_params("arbitrary"),
        name="kv_proj",
    )(*args)
    return (outs[0], outs[1]), (outs[2] if cast_src is not None else None)


def _attn_kernel(q_ref, *refs, tq):
    o_ref = refs[-1]
    kv_refs = refs[:-1]
    scale = ATTN_SCALE * math.log2(math.e)
    steps = []
    for s in range(0, len(kv_refs), 2):
        n_keys = kv_refs[s].shape[2]
        tk = min(KEY_CHUNK, n_keys)
        steps += [(kv_refs[s], kv_refs[s + 1], j, tk) for j in range(0, n_keys, tk)]
    for hd in range(q_ref.shape[1]):
        for r in range(0, q_ref.shape[2], tq):
            q = q_ref[0, hd, r:r + tq, :]
            m = acc = None
            for k_ref, v_ref, j, n in steps:
                sc = _dot_nt(q, k_ref[0, hd, j:j + n, :])
                m_new = jnp.max(sc, axis=-1, keepdims=True)
                if m is not None:
                    m_new = jnp.maximum(m, m_new)
                p = jnp.exp2((sc - m_new) * scale).astype(BF16)
                pv = _dot(p, jnp.concatenate([v_ref[0, hd, j:j + n, :], jnp.ones((n, LANES), BF16)], axis=1))
                acc = pv if m is None else jnp.exp2((m - m_new) * scale) * acc + pv
                m = m_new
            o_ref[0, r:r + tq, hd * V_DIM:(hd + 1) * V_DIM] = (acc[:, :V_DIM] / acc[:, V_DIM:]).astype(BF16)


def _attention(q, kvs, tq, heads_per_step):
    batch, _, seq, _ = q.shape
    hps = heads_per_step
    head_spec = lambda rows, width: pl.BlockSpec((1, hps, rows, width), lambda b, hg: (b, hg, 0, 0))
    in_specs = [head_spec(seq, QK_STORE)]
    args = [q]
    for k, v in kvs:
        in_specs += [head_spec(k.shape[2], QK_STORE), head_spec(v.shape[2], V_DIM)]
        args += [k, v]
    return pl.pallas_call(
        functools.partial(_attn_kernel, tq=tq),
        grid=(batch, N_HEADS // hps),
        in_specs=in_specs,
        out_specs=pl.BlockSpec((1, seq, hps * V_DIM), lambda b, hg: (b, 0, hg)),
        out_shape=jax.ShapeDtypeStruct((batch, seq, N_HEADS * V_DIM), BF16),
        compiler_params=_params("arbitrary", "arbitrary"),
        name="attention",
    )(*args)


def _merge_kernel(h_ref, pa_ref, pb_ref, oc_ref, wg0_ref, wg1_ref, wg2_ref, wa_ref, wb_ref, wc_ref, o_ref):
    branches = ((wg0_ref, pa_ref, wa_ref), (wg1_ref, pb_ref, wb_ref), (wg2_ref, oc_ref, wc_ref))
    weights = [w_ref[...].astype(BF16) for _, _, w_ref in branches]
    for rows in _row_blocks(o_ref.shape[0]):
        merged = None
        for (wg_ref, y_ref, _), w in zip(branches, weights):
            term = jax.nn.sigmoid(_dot_nt(h_ref[rows, :], wg_ref[...])) * _dot(y_ref[rows, :], w)
            merged = term if merged is None else merged + term
        o_ref[rows, :] = merged.astype(BF16)


def _merge(h, pa, pb, oc, wg_t, wba, wbb, wbc, layer, tm, tn):
    rows = h.shape[0]
    nb = D_MODEL // tn
    row_spec = lambda width: pl.BlockSpec((tm, width), lambda i, j: (i, 0))
    gate_spec = lambda k: pl.BlockSpec((None, tn, D_MODEL), lambda i, j: (layer, k * nb + j, 0))
    col_spec = lambda depth: pl.BlockSpec((None, depth, tn), lambda i, j: (layer, 0, j))
    return pl.pallas_call(
        _merge_kernel,
        grid=(rows // tm, nb),
        in_specs=[
            row_spec(D_MODEL), row_spec(POOL_WIDTH), row_spec(CONV_WIDTH), row_spec(N_HEADS * V_DIM),
            gate_spec(0), gate_spec(1), gate_spec(2),
            col_spec(POOL_WIDTH), col_spec(CONV_WIDTH), col_spec(N_HEADS * V_DIM),
        ],
        out_specs=pl.BlockSpec((tm, tn), lambda i, j: (i, j)),
        out_shape=jax.ShapeDtypeStruct((rows, D_MODEL), BF16),
        compiler_params=_params("arbitrary", "arbitrary"),
        name="merge",
    )(h, pa, pb, oc, wg_t, wg_t, wg_t, wba, wbb, wbc)


def _matmul_residual_kernel(a_ref, w_ref, res_ref, gate_ref, o_ref):
    w = w_ref[...]
    for rows in _row_blocks(o_ref.shape[0]):
        o_ref[rows, :] = res_ref[rows, :] + gate_ref[0] * _dot(a_ref[rows, :], w)


def _matmul_residual(a, w, res, mods, seg_of_tile, layer, which_gate, tm, tn):
    rows, depth = a.shape
    return pl.pallas_call(
        _matmul_residual_kernel,
        grid=(rows // tm, D_MODEL // tn),
        in_specs=[
            pl.BlockSpec((tm, depth), lambda i, j: (i, 0)),
            pl.BlockSpec((depth, tn), lambda i, j: (0, j)),
            pl.BlockSpec((tm, tn), lambda i, j: (i, j)),
            pl.BlockSpec((1, 1, tn), lambda i, j: (_mod_row(layer, seg_of_tile(i), which_gate), 0, j)),
        ],
        out_specs=pl.BlockSpec((tm, tn), lambda i, j: (i, j)),
        out_shape=jax.ShapeDtypeStruct((rows, D_MODEL), F32),
        compiler_params=_params("arbitrary", "arbitrary"),
        name="matmul_residual",
    )(a, w, res, mods)


def _matmul_residual_norm_kernel(a_ref, w_ref, res_ref, gate_ref, g_ref, sh_ref, sc_ref, x_ref, h_ref):
    a = a_ref[...]
    for n in range(0, D_MODEL, MXU_ROWS):
        cols = slice(n, n + MXU_ROWS)
        x_ref[:, cols] = res_ref[:, cols] + gate_ref[0, :, cols] * _dot(a, w_ref[:, cols])
    x = x_ref[...]
    y = x * lax.rsqrt(jnp.mean(x * x, axis=-1, keepdims=True) + EPS) * g_ref[0]
    h_ref[...] = (y * (1 + sc_ref[0]) + sh_ref[0]).astype(BF16)


def _matmul_residual_norm(a, w, res, mods, g, seg_of_tile, layer, which_gate, which_shift, tm):
    rows, depth = a.shape
    mod = lambda which: pl.BlockSpec((1, 1, D_MODEL), lambda i: (_mod_row(layer, seg_of_tile(i), which), 0, 0))
    row_spec = lambda width: pl.BlockSpec((tm, width), lambda i: (i, 0))
    return pl.pallas_call(
        _matmul_residual_norm_kernel,
        grid=(rows // tm,),
        in_specs=[
            row_spec(depth),
            pl.BlockSpec((depth, D_MODEL), lambda i: (0, 0)),
            row_spec(D_MODEL),
            mod(which_gate),
            pl.BlockSpec((1, 1, D_MODEL), lambda i: (layer, 0, 0)),
            mod(which_shift), mod(which_shift + 1),
        ],
        out_specs=[row_spec(D_MODEL), row_spec(D_MODEL)],
        out_shape=[jax.ShapeDtypeStruct((rows, D_MODEL), F32), jax.ShapeDtypeStruct((rows, D_MODEL), BF16)],
        compiler_params=_params("arbitrary"),
        name="matmul_residual_norm",
    )(a, w, res, mods, g.reshape(DEPTH, 1, D_MODEL), mods, mods)


def _ffn_up_kernel(h_ref, wu_ref, wv_ref, cw_ref, *rest, seq, sub):
    if len(rest) == 3:
        cast_src_ref, o_ref, cast_dst_ref = rest
        cast_dst_ref[...] = cast_src_ref[...].astype(BF16)
    else:
        o_ref, = rest
    for n in range(0, o_ref.shape[1], sub):
        cols = slice(n, n + sub)
        cu = _conv3(_dot_rows(h_ref, wu_ref[:, cols].astype(BF16)), cw_ref.at[:, cols], seq)
        gate = _dot_rows(h_ref, wv_ref[:, cols].astype(BF16))
        o_ref[:, cols] = (cu * jax.nn.sigmoid(cu) * gate).astype(BF16)


def _ffn_up(h, w_up, conv, layer, seq, tm, tn, cast_src=None):
    rows = h.shape[0]
    nb = D_FF // tn
    in_specs = [
        pl.BlockSpec((tm, D_MODEL), lambda b, f: (b, 0)),
        pl.BlockSpec((None, D_MODEL, tn), lambda b, f: (layer, 0, f)),
        pl.BlockSpec((None, D_MODEL, tn), lambda b, f: (layer, 0, nb + f)),
        pl.BlockSpec((None, 3, tn), lambda b, f: (layer, 0, f)),
    ]
    out_specs = [pl.BlockSpec((tm, tn), lambda b, f: (b, f))]
    out_shape = [jax.ShapeDtypeStruct((rows, D_FF), BF16)]
    args = [h, w_up, w_up, conv]
    if cast_src is not None:
        src_spec, dst_spec, dst_shape = _cast_rider(cast_src, layer, (rows // tm) * nb, lambda b, f: b * nb + f)
        in_specs.append(src_spec)
        out_specs.append(dst_spec)
        out_shape.append(dst_shape)
        args.append(cast_src)
    outs = pl.pallas_call(
        functools.partial(_ffn_up_kernel, seq=seq, sub=min(tn, 256)),
        grid=(rows // tm, nb),
        in_specs=in_specs,
        out_specs=out_specs,
        out_shape=out_shape,
        compiler_params=_params("arbitrary", "arbitrary"),
        name="ffn_up",
    )(*args)
    return outs[0], (outs[1] if cast_src is not None else None)


_ROPE_SWAP = np.concatenate([np.arange(16, 32), np.arange(0, 16), np.arange(48, 64), np.arange(32, 48)])


def _rope_cos_sin(length):
    rows = length // GRID_W
    row = np.repeat(np.arange(rows, dtype=np.int32), GRID_W).astype(np.float32)
    col = np.tile(np.arange(GRID_W, dtype=np.int32), rows).astype(np.float32)
    inv = np.float32(ROPE_THETA) ** (-np.arange(0, ROPE_AXIS, 2, dtype=np.float32) / np.float32(ROPE_AXIS))
    ang_r = row[:, None] * inv[None]
    ang_c = col[:, None] * inv[None]
    cr, sr, cc, sc = np.cos(ang_r), np.sin(ang_r), np.cos(ang_c), np.sin(ang_c)
    return np.concatenate([cr, cr, cc, cc], axis=-1), np.concatenate([-sr, sr, -sc, sc], axis=-1)


def _identity_cos_sin(length):
    return np.ones((length, QK_ROPE), np.float32), np.zeros((length, QK_ROPE), np.float32)


def _rope_tables(cos, sin):
    return (jnp.asarray(np.concatenate([cos, cos, sin, sin], axis=-1), F32),
            jnp.asarray(np.concatenate([cos, sin], axis=-1), F32))


def _prepared_weights(w_in_t, w_uq, w_ukv, q_head_g, k_head_g):
    uq = jnp.transpose(w_uq, (0, 2, 1, 3)).reshape(DEPTH, N_HEADS // 2, 2, Q_LORA, QK_DIM)
    even, odd = uq[:, :, 0], uq[:, :, 1]
    swapped = lambda w: w[..., QK_NOPE:][..., _ROPE_SWAP]
    q_rope, k_rope_g = q_head_g[:, QK_NOPE:], k_head_g[:, QK_NOPE:]
    q_swap, k_swap = q_rope[:, _ROPE_SWAP], k_rope_g[:, _ROPE_SWAP]
    return dict(
        wg_t=w_in_t[:, OFF_G:, :].astype(BF16),
        wuq=jnp.concatenate([even[..., :QK_NOPE], odd[..., :QK_NOPE], even[..., QK_NOPE:], odd[..., QK_NOPE:],
                             swapped(even), swapped(odd)], axis=-1).astype(BF16),
        wukv=w_ukv.reshape(DEPTH, KV_LORA, N_HEADS * (QK_NOPE + V_DIM)).astype(BF16),
        q_gn=q_head_g[:, None, :QK_NOPE], k_gn=k_head_g[:, None, :QK_NOPE],
        q_g2=jnp.concatenate([q_rope, q_rope, q_swap, q_swap], axis=-1)[:, None],
        k_g2=jnp.concatenate([k_rope_g, k_swap], axis=-1)[:, None],
    )


def _tiles(batch, seq):
    rows = batch * seq
    return dict(
        tm=min(rows, 1024),
        seq_tm=seq if seq >= 1024 else rows,
        full_row_tm=min(rows, 512),
        proj_tm=min(seq, 512),
        q_tm=min(seq, 512),
        tq=min(seq, 512),
        attn_heads=max(1, min(N_HEADS, 4096 // seq)),
        narrow_tn=256,
        wide_tn=512,
    )


def kernel(x, c, ctx, c_ctx, norm1_g, norm2_g, w_mod, b_mod, w_in, pool_w, pool_scale, conv_w, q_lora_g, w_uq,
           kv_lora_g, w_ukv, q_head_g, k_head_g, w_branch_a, w_branch_b, w_branch_c, w_out, w_ffn_up, ffn_conv,
           w_ffn_down):
    batch, seq, _ = x.shape
    ctx_len = ctx.shape[1]
    assert batch + 1 <= MOD_ROWS
    cv = jnp.concatenate([c, c_ctx[None], jnp.zeros((MOD_ROWS - batch - 1, D_MODEL), F32)], axis=0)
    mods = _modulation(cv, w_mod, b_mod)
    w_in_t = jnp.swapaxes(w_in, 1, 2)
    pw = _prepared_weights(w_in_t, w_uq, w_ukv, q_head_g, k_head_g)

    def kv_proj(h, layer, cs, n_tok, tiles, bf16_w):
        kv, cast = _kv_proj(h, w_in_t, kv_lora_g, pw['wukv'], pw['k_gn'], pw['k_g2'], cs[1], layer, batch, n_tok,
                            tiles['proj_tm'], cast_src=None if 'w_out' in bf16_w else w_out)
        bf16_w.setdefault('w_out', cast)
        return kv

    def stream_layer(xs, layer, seg_of_tile, cs, n_tok, ctx_kv, tiles, bf16_w):
        tm, seq_tm = tiles['tm'], tiles['seq_tm']
        h = _norm_mod(xs, norm1_g, mods, seg_of_tile(tm), layer, 0, tm)
        pa = _branch_a(h, w_in_t, pool_w, pool_scale, layer, n_tok)
        pb = _branch_b(h, w_in_t, conv_w, layer, n_tok, seq_tm, tiles['narrow_tn'])
        q = _q_proj(h, w_in_t, q_lora_g, pw['wuq'], pw['q_gn'], pw['q_g2'], cs[0], layer, batch, n_tok,
                    tiles['q_tm'])
        kv = kv_proj(h, layer, cs, n_tok, tiles, bf16_w)
        kvs = [kv] if ctx_kv is None else [kv, ctx_kv]
        oc = _attention(q, kvs, tiles['tq'], tiles['attn_heads']).reshape(batch * n_tok, N_HEADS * V_DIM)
        merged = _merge(h, pa, pb, oc, pw['wg_t'], w_branch_a, w_branch_b, w_branch_c, layer, tm,
                        tiles['narrow_tn'])
        full_tm = tiles['full_row_tm']
        xs, h2 = _matmul_residual_norm(merged, bf16_w['w_out'], xs, mods, norm2_g, seg_of_tile(full_tm), layer,
                                       2, 3, full_tm)
        hidden, cast = _ffn_up(h2, w_ffn_up, ffn_conv, layer, n_tok, seq_tm, tiles['wide_tn'],
                               cast_src=None if 'w_down' in bf16_w else w_ffn_down)
        bf16_w.setdefault('w_down', cast)
        xs = _matmul_residual(hidden, bf16_w['w_down'], xs, mods, seg_of_tile(tm), layer, 5, tm, tiles['wide_tn'])
        return xs, kv

    cs_x = _rope_tables(*_rope_cos_sin(seq))
    cs_c = _rope_tables(*_identity_cos_sin(ctx_len))
    x_tiles = _tiles(batch, seq)
    c_tiles = _tiles(batch, ctx_len)
    x_seg = lambda tm: (lambda i: i // (seq // tm))
    c_seg = lambda tm: (lambda i: batch)

    xs = x.reshape(batch * seq, D_MODEL)
    cx = ctx.reshape(batch * ctx_len, D_MODEL)
    for layer in range(DEPTH):
        bf16_w = {}
        if layer == DEPTH - 1:
            hc = _norm_mod(cx, norm1_g, mods, c_seg(c_tiles['tm']), layer, 0, c_tiles['tm'])
            ctx_kv = kv_proj(hc, layer, cs_c, ctx_len, c_tiles, bf16_w)
        else:
            cx, ctx_kv = stream_layer(cx, layer, c_seg, cs_c, ctx_len, None, c_tiles, bf16_w)
        xs, _ = stream_layer(xs, layer, x_seg, cs_x, seq, tuple(ctx_kv), x_tiles, bf16_w)
    return xs.reshape(batch, seq, D_MODEL)
```

```python
import functools
import math

import jax
import jax.numpy as jnp
import numpy as np
from jax import lax
from jax.experimental import pallas as pl
from jax.experimental.pallas import tpu as pltpu

D_MODEL = 2048
DEPTH = 2
GRID_W = 64
POOL_WINDOWS = (2, 4, 8, 16)
POOL_GROUP = D_MODEL // 8
POOL_WIDTH = POOL_GROUP * len(POOL_WINDOWS)
CONV_WIDTH = D_MODEL // 2
N_HEADS = 16
QK_NOPE = 128
QK_ROPE = 64
ROPE_AXIS = QK_ROPE // 2
V_DIM = 128
QK_DIM = QK_NOPE + QK_ROPE
Q_LORA = 512
KV_LORA = 512
ROPE_THETA = 10000.0
ATTN_SCALE = QK_DIM ** -0.5
D_FF = 5632
N_BRANCH = 3
EPS = 1e-6
OFF_A = 0
OFF_B = OFF_A + POOL_WIDTH
OFF_Q = OFF_B + 3 * CONV_WIDTH
OFF_KV = OFF_Q + Q_LORA
OFF_KROPE = OFF_KV + KV_LORA
OFF_G = OFF_KROPE + QK_ROPE

LANES = 128
QK_STORE = 2 * LANES
MXU_ROWS = 512
PROJ_ROWS = 512
POOL_PAD = 16
MOD_ROWS = 8
N_MOD = 6
VMEM_LIMIT = 56 * 1024 * 1024
KEY_CHUNK = 512

ONE_BUFFER = pl.Buffered(1)

BF16 = jnp.bfloat16
F32 = jnp.float32


def _dot(a, b):
    return jnp.dot(a, b, preferred_element_type=F32)


def _dot_nt(a, b_t):
    return lax.dot_general(a, b_t, (((1,), (1,)), ((), ())), preferred_element_type=F32)


def _dot_rows(a_ref, w, dot=_dot):
    rows = a_ref.shape[0]
    if rows <= MXU_ROWS:
        return dot(a_ref[...], w)
    return jnp.concatenate([dot(a_ref[r:r + MXU_ROWS, :], w) for r in range(0, rows, MXU_ROWS)], axis=0)


def _row_blocks(rows, block=MXU_ROWS):
    return [slice(r, min(r + block, rows)) for r in range(0, rows, block)]


def _cast_rider(src, layer, n_steps, step_index):
    rows, width = src.shape[1:]
    slab = rows // n_steps
    assert slab * n_steps == rows and slab % 16 == 0
    in_spec = pl.BlockSpec((None, slab, width), lambda *g: (layer, step_index(*g), 0))
    out_spec = pl.BlockSpec((slab, width), lambda *g: (step_index(*g), 0))
    return in_spec, out_spec, jax.ShapeDtypeStruct((rows, width), BF16)


def _in_proj(h_ref, wt_ref):
    return _dot_rows(h_ref, wt_ref[...].astype(BF16), _dot_nt)


def _params(*sem):
    return pltpu.CompilerParams(dimension_semantics=sem, vmem_limit_bytes=VMEM_LIMIT)


def _mod_row(layer, seg, which):
    return (layer * MOD_ROWS + seg) * N_MOD + which


def _mod_kernel(cv_ref, w_ref, b_ref, o_ref):
    a = cv_ref[...]
    a = a * jax.nn.sigmoid(a)
    o_ref[...] = _dot(a.astype(BF16), w_ref[...].astype(BF16)) + b_ref[...]


def _modulation(cv, w_mod, b_mod):
    tn = 1024
    n_out = N_MOD * D_MODEL
    out = pl.pallas_call(
        _mod_kernel,
        grid=(DEPTH, n_out // tn),
        in_specs=[
            pl.BlockSpec((MOD_ROWS, D_MODEL), lambda l, j: (0, 0)),
            pl.BlockSpec((None, D_MODEL, tn), lambda l, j: (l, 0, j)),
            pl.BlockSpec((None, 1, tn), lambda l, j: (l, 0, j)),
        ],
        out_specs=pl.BlockSpec((None, MOD_ROWS, tn), lambda l, j: (l, 0, j)),
        out_shape=jax.ShapeDtypeStruct((DEPTH, MOD_ROWS, n_out), F32),
        compiler_params=_params("arbitrary", "arbitrary"),
        name="modulation",
    )(cv, w_mod, b_mod.reshape(DEPTH, 1, n_out))
    return out.reshape(DEPTH * MOD_ROWS * N_MOD, 1, D_MODEL)


def _norm_mod_kernel(x_ref, g_ref, sh_ref, sc_ref, o_ref):
    x = x_ref[...]
    y = x * lax.rsqrt(jnp.mean(x * x, axis=-1, keepdims=True) + EPS)
    y = y * g_ref[0]
    o_ref[...] = (y * (1 + sc_ref[0]) + sh_ref[0]).astype(BF16)


def _norm_mod(x, g, mods, seg_of_tile, layer, which_shift, tm):
    rows = x.shape[0]
    return pl.pallas_call(
        _norm_mod_kernel,
        grid=(rows // tm,),
        in_specs=[
            pl.BlockSpec((tm, D_MODEL), lambda i: (i, 0)),
            pl.BlockSpec((1, 1, D_MODEL), lambda i: (layer, 0, 0)),
            pl.BlockSpec((1, 1, D_MODEL), lambda i: (_mod_row(layer, seg_of_tile(i), which_shift), 0, 0)),
            pl.BlockSpec((1, 1, D_MODEL), lambda i: (_mod_row(layer, seg_of_tile(i), which_shift + 1), 0, 0)),
        ],
        out_specs=pl.BlockSpec((tm, D_MODEL), lambda i: (i, 0)),
        out_shape=jax.ShapeDtypeStruct((rows, D_MODEL), BF16),
        compiler_params=_params("arbitrary"),
        name="norm_mod",
    )(x, g.reshape(DEPTH, 1, D_MODEL), mods, mods)


def _branch_a_kernel(h_ref, *refs, seq):
    n_win = len(POOL_WINDOWS)
    wa_refs = refs[:n_win]
    pw_ref, ps_ref, o_ref, ext_ref = refs[n_win:]
    n_ext = seq + 2 * POOL_PAD
    zeros = jnp.zeros((POOL_PAD, POOL_GROUP), F32)
    ext_ref[0:POOL_PAD, :] = zeros
    ext_ref[seq + POOL_PAD:n_ext, :] = zeros
    t = lax.broadcasted_iota(jnp.int32, (seq, 1), 0)
    for g, w in enumerate(POOL_WINDOWS):
        cols = slice(g * POOL_GROUP, (g + 1) * POOL_GROUP)
        a = _in_proj(h_ref, wa_refs[g])
        ext_ref[POOL_PAD:seq + POOL_PAD, :] = a
        e = ext_ref[...]
        c = e + pltpu.roll(e, 1, 0)
        half = 1
        while 2 * half < w:
            c = pltpu.roll(c, half, 0) + pltpu.roll(c, n_ext - half, 0)
            half *= 2
        cnt = (jnp.minimum(t + w // 2, seq) - jnp.maximum(t - w // 2, 0)).astype(F32)
        pooled = c[POOL_PAD:seq + POOL_PAD, :] / cnt - a
        mixed = _dot(pooled.astype(BF16), pw_ref[g].astype(BF16)) * ps_ref[0, :, cols]
        o_ref[:, cols] = mixed.astype(BF16)


def _branch_a(h, w_in_t, pool_w, pool_scale, layer, seq):
    rows = h.shape[0]
    n_win = len(POOL_WINDOWS)
    group_rows = lambda g: pl.BlockSpec((None, POOL_GROUP, D_MODEL), lambda b: (layer, OFF_A // POOL_GROUP + g, 0))
    return pl.pallas_call(
        functools.partial(_branch_a_kernel, seq=seq),
        grid=(rows // seq,),
        in_specs=[pl.BlockSpec((seq, D_MODEL), lambda b: (b, 0))] + [group_rows(g) for g in range(n_win)] + [
            pl.BlockSpec((None, n_win, POOL_GROUP, POOL_GROUP), lambda b: (layer, 0, 0, 0)),
            pl.BlockSpec((1, 1, POOL_WIDTH), lambda b: (layer, 0, 0)),
        ],
        out_specs=pl.BlockSpec((seq, POOL_WIDTH), lambda b: (b, 0)),
        out_shape=jax.ShapeDtypeStruct((rows, POOL_WIDTH), BF16),
        scratch_shapes=[pltpu.VMEM((seq + 2 * POOL_PAD, POOL_GROUP), F32)],
        compiler_params=_params("arbitrary"),
        name="branch_a",
    )(h, *([w_in_t] * n_win), pool_w, pool_scale.reshape(DEPTH, 1, POOL_WIDTH))


def _conv3(u, cw_ref, seq):
    rows = u.shape[0]
    t = lax.broadcasted_iota(jnp.int32, (rows, 1), 0) % seq
    prev = jnp.where(t == 0, 0.0, pltpu.roll(u, 1, 0))
    nxt = jnp.where(t == seq - 1, 0.0, pltpu.roll(u, rows - 1, 0))
    return prev * cw_ref[0:1, :] + u * cw_ref[1:2, :] + nxt * cw_ref[2:3, :]


def _branch_b_kernel(h_ref, wgb_ref, wgc_ref, wx_ref, cw_ref, o_ref, *, seq):
    u = _in_proj(h_ref, wgc_ref) * _in_proj(h_ref, wx_ref)
    o_ref[...] = (_in_proj(h_ref, wgb_ref) * _conv3(u, cw_ref, seq)).astype(BF16)


def _branch_b(h, w_in_t, conv_w, layer, seq, tm, tn):
    rows = h.shape[0]
    nb = CONV_WIDTH // tn
    first = OFF_B // tn
    w_spec = lambda k: pl.BlockSpec((None, tn, D_MODEL), lambda b, c: (layer, first + k * nb + c, 0))
    return pl.pallas_call(
        functools.partial(_branch_b_kernel, seq=seq),
        grid=(rows // tm, nb),
        in_specs=[
            pl.BlockSpec((tm, D_MODEL), lambda b, c: (b, 0)),
            w_spec(0), w_spec(1), w_spec(2),
            pl.BlockSpec((None, 3, tn), lambda b, c: (layer, 0, c)),
        ],
        out_specs=pl.BlockSpec((tm, tn), lambda b, c: (b, c)),
        out_shape=jax.ShapeDtypeStruct((rows, CONV_WIDTH), BF16),
        compiler_params=_params("arbitrary", "arbitrary"),
        name="branch_b",
    )(h, w_in_t, w_in_t, w_in_t, conv_w)


def _lo_lanes():
    return lax.broadcasted_iota(jnp.int32, (1, LANES), 1) < QK_ROPE


def _q_proj_kernel(h_ref, wq_ref, lg_ref, wuq_ref, gn_ref, g2_ref, cs_ref, o_ref):
    wq = wq_ref[...].astype(BF16)
    lo = _lo_lanes()
    for rows in _row_blocks(h_ref.shape[0], PROJ_ROWS):
        zq = _dot_nt(h_ref[rows, :], wq)
        cq = zq * lax.rsqrt(jnp.mean(zq * zq, axis=-1, keepdims=True) + EPS) * lg_ref[0]
        cq = cq.astype(BF16)
        gain_table = g2_ref[0] * cs_ref[rows, :]
        for pair in range(N_HEADS // 2):
            y = _dot(cq, wuq_ref[pair])
            nope = (y[:, 0:LANES], y[:, LANES:2 * LANES])
            yr, ys = y[:, 2 * LANES:3 * LANES], y[:, 3 * LANES:]
            sq_r = yr * yr
            rot = yr * gain_table[:, :LANES] + ys * gain_table[:, LANES:]
            for par in range(2):
                sq = nope[par] * nope[par] + jnp.where(lo if par == 0 else ~lo, sq_r, 0.0)
                r = lax.rsqrt(jnp.sum(sq, axis=-1, keepdims=True) / QK_DIM + EPS)
                hd = 2 * pair + par
                o_ref[0, hd, rows, 0:LANES] = (nope[par] * r * gn_ref[0]).astype(BF16)
                o_ref[0, hd, rows, LANES:QK_STORE] = jnp.where(lo if par == 0 else ~lo, rot * r, 0.0).astype(BF16)


def _q_proj(h, w_in_t, lora_g, wuq, gn, g2, cs, layer, batch, seq, tm):
    nt = seq // tm
    vec = lambda width: pl.BlockSpec((1, 1, width), lambda i: (layer, 0, 0))
    return pl.pallas_call(
        _q_proj_kernel,
        grid=(batch * nt,),
        in_specs=[
            pl.BlockSpec((tm, D_MODEL), lambda i: (i, 0)),
            pl.BlockSpec((None, Q_LORA, D_MODEL), lambda i: (layer, OFF_Q // Q_LORA, 0),
                         pipeline_mode=ONE_BUFFER),
            vec(Q_LORA),
            pl.BlockSpec((None, N_HEADS // 2, Q_LORA, 4 * LANES), lambda i: (layer, 0, 0, 0),
                         pipeline_mode=ONE_BUFFER),
            vec(LANES), vec(2 * LANES),
            pl.BlockSpec((tm, 2 * LANES), lambda i: (i % nt, 0)),
        ],
        out_specs=pl.BlockSpec((1, N_HEADS, tm, QK_STORE), lambda i: (i // nt, 0, i % nt, 0)),
        out_shape=jax.ShapeDtypeStruct((batch, N_HEADS, seq, QK_STORE), BF16),
        compiler_params=_params("arbitrary"),
        name="q_proj",
    )(h, w_in_t, lora_g.reshape(DEPTH, 1, Q_LORA), wuq, gn, g2, cs)


def _kv_proj_kernel(h_ref, wkv_ref, wkr_ref, lg_ref, wukv_ref, gn_ref, g2_ref, cs_ref, *rest):
    if len(rest) == 4:
        cast_src_ref, k_ref, v_ref, cast_dst_ref = rest
        cast_dst_ref[...] = cast_src_ref[...].astype(BF16)
    else:
        k_ref, v_ref = rest
    wkv = wkv_ref[...].astype(BF16)
    q4 = QK_ROPE // 4
    wkr = jnp.concatenate([wkr_ref[...]] + [wkr_ref[s * q4:(s + 1) * q4, :] for s in (1, 0, 3, 2)], axis=0)
    wkr = wkr.astype(BF16)
    lo = _lo_lanes()
    for rows in _row_blocks(h_ref.shape[0], PROJ_ROWS):
        h = h_ref[rows, :]
        zc = _dot_nt(h, wkv)
        kr = _dot_nt(h, wkr)
        ckv = zc * lax.rsqrt(jnp.mean(zc * zc, axis=-1, keepdims=True) + EPS) * lg_ref[0]
        ckv = ckv.astype(BF16)
        ss_rope = jnp.sum(jnp.where(lo, kr * kr, 0.0), axis=-1, keepdims=True)
        u = kr * (g2_ref[0] * cs_ref[rows, :])
        rot = u + pltpu.roll(u, LANES // 2, 1)
        rope = (jnp.where(lo, rot, 0.0), jnp.where(lo, 0.0, rot))
        for hd in range(N_HEADS):
            kv = _dot(ckv, wukv_ref[:, hd * 2 * LANES:(hd + 1) * 2 * LANES])
            kn, v = kv[:, :QK_NOPE], kv[:, QK_NOPE:]
            ss = jnp.sum(kn * kn, axis=-1, keepdims=True) + ss_rope
            r = lax.rsqrt(ss / QK_DIM + EPS)
            k_ref[0, hd, rows, 0:LANES] = (kn * r * gn_ref[0]).astype(BF16)
            k_ref[0, hd, rows, LANES:QK_STORE] = (rope[hd % 2] * r).astype(BF16)
            v_ref[0, hd, rows, :] = v.astype(BF16)


def _kv_proj(h, w_in_t, lora_g, wukv, gn, g2, cs, layer, batch, seq, tm, cast_src=None):
    nt = seq // tm
    vec = lambda width: pl.BlockSpec((1, 1, width), lambda i: (layer, 0, 0))
    in_specs = [
        pl.BlockSpec((tm, D_MODEL), lambda i: (i, 0)),
        pl.BlockSpec((None, KV_LORA, D_MODEL), lambda i: (layer, OFF_KV // KV_LORA, 0),
                     pipeline_mode=ONE_BUFFER),
        pl.BlockSpec((None, QK_ROPE, D_MODEL), lambda i: (layer, OFF_KROPE // QK_ROPE, 0),
                     pipeline_mode=ONE_BUFFER),
        vec(KV_LORA),
        pl.BlockSpec((None, KV_LORA, N_HEADS * 2 * LANES), lambda i: (layer, 0, 0),
                     pipeline_mode=ONE_BUFFER),
        vec(LANES), vec(LANES),
        pl.BlockSpec((tm, LANES), lambda i: (i % nt, 0)),
    ]
    out_specs = [
        pl.BlockSpec((1, N_HEADS, tm, QK_STORE), lambda i: (i // nt, 0, i % nt, 0)),
        pl.BlockSpec((1, N_HEADS, tm, V_DIM), lambda i: (i // nt, 0, i % nt, 0)),
    ]
    out_shape = [
        jax.ShapeDtypeStruct((batch, N_HEADS, seq, QK_STORE), BF16),
        jax.ShapeDtypeStruct((batch, N_HEADS, seq, V_DIM), BF16),
    ]
    args = [h, w_in_t, w_in_t, lora_g.reshape(DEPTH, 1, KV_LORA), wukv, gn, g2, cs]
    if cast_src is not None:
        src_spec, dst_spec, dst_shape = _cast_rider(cast_src, layer, batch * nt, lambda i: i)
        in_specs.append(src_spec)
        out_specs.append(dst_spec)
        out_shape.append(dst_shape)
        args.append(cast_src)
    outs = pl.pallas_call(
        _kv_proj_kernel,
        grid=(batch * nt,),
        in_specs=in_specs,
        out_specs=out_specs,
        out_shape=out_shape,
        compiler_params=_params("arbitrary"),
        name="kv_proj",
    )(*args)
    return (outs[0], outs[1]), (outs[2] if cast_src is not None else None)


def _attn_kernel(q_ref, *refs, tq):
    o_ref = refs[-1]
    kv_refs = refs[:-1]
    scale = ATTN_SCALE * math.log2(math.e)
    steps = []
    for s in range(0, len(kv_refs), 2):
        n_keys = kv_refs[s].shape[2]
        tk = min(KEY_CHUNK, n_keys)
        steps += [(kv_refs[s], kv_refs[s + 1], j, tk) for j in range(0, n_keys, tk)]
    for hd in range(q_ref.shape[1]):
        for r in range(0, q_ref.shape[2], tq):
            q = q_ref[0, hd, r:r + tq, :]
            m = acc = None
            for k_ref, v_ref, j, n in steps:
                sc = _dot_nt(q, k_ref[0, hd, j:j + n, :])
                m_new = jnp.max(sc, axis=-1, keepdims=True)
                if m is not None:
                    m_new = jnp.maximum(m, m_new)
                p = jnp.exp2((sc - m_new) * scale).astype(BF16)
                pv = _dot(p, jnp.concatenate([v_ref[0, hd, j:j + n, :], jnp.ones((n, LANES), BF16)], axis=1))
                acc = pv if m is None else jnp.exp2((m - m_new) * scale) * acc + pv
                m = m_new
            o_ref[0, r:r + tq, hd * V_DIM:(hd + 1) * V_DIM] = (acc[:, :V_DIM] / acc[:, V_DIM:]).astype(BF16)


def _attention(q, kvs, tq, heads_per_step):
    batch, _, seq, _ = q.shape
    hps = heads_per_step
    head_spec = lambda rows, width: pl.BlockSpec((1, hps, rows, width), lambda b, hg: (b, hg, 0, 0))
    in_specs = [head_spec(seq, QK_STORE)]
    args = [q]
    for k, v in kvs:
        in_specs += [head_spec(k.shape[2], QK_STORE), head_spec(v.shape[2], V_DIM)]
        args += [k, v]
    return pl.pallas_call(
        functools.partial(_attn_kernel, tq=tq),
        grid=(batch, N_HEADS // hps),
        in_specs=in_specs,
        out_specs=pl.BlockSpec((1, seq, hps * V_DIM), lambda b, hg: (b, 0, hg)),
        out_shape=jax.ShapeDtypeStruct((batch, seq, N_HEADS * V_DIM), BF16),
        compiler_params=_params("arbitrary", "arbitrary"),
        name="attention",
    )(*args)


def _merge_kernel(h_ref, pa_ref, pb_ref, oc_ref, wg0_ref, wg1_ref, wg2_ref, wa_ref, wb_ref, wc_ref, o_ref):
    branches = ((wg0_ref, pa_ref, wa_ref), (wg1_ref, pb_ref, wb_ref), (wg2_ref, oc_ref, wc_ref))
    weights = [w_ref[...].astype(BF16) for _, _, w_ref in branches]
    for rows in _row_blocks(o_ref.shape[0]):
        merged = None
        for (wg_ref, y_ref, _), w in zip(branches, weights):
            term = jax.nn.sigmoid(_dot_nt(h_ref[rows, :], wg_ref[...])) * _dot(y_ref[rows, :], w)
            merged = term if merged is None else merged + term
        o_ref[rows, :] = merged.astype(BF16)


def _merge(h, pa, pb, oc, wg_t, wba, wbb, wbc, layer, tm, tn):
    rows = h.shape[0]
    nb = D_MODEL // tn
    row_spec = lambda width: pl.BlockSpec((tm, width), lambda i, j: (i, 0))
    gate_spec = lambda k: pl.BlockSpec((None, tn, D_MODEL), lambda i, j: (layer, k * nb + j, 0))
    col_spec = lambda depth: pl.BlockSpec((None, depth, tn), lambda i, j: (layer, 0, j))
    return pl.pallas_call(
        _merge_kernel,
        grid=(rows // tm, nb),
        in_specs=[
            row_spec(D_MODEL), row_spec(POOL_WIDTH), row_spec(CONV_WIDTH), row_spec(N_HEADS * V_DIM),
            gate_spec(0), gate_spec(1), gate_spec(2),
            col_spec(POOL_WIDTH), col_spec(CONV_WIDTH), col_spec(N_HEADS * V_DIM),
        ],
        out_specs=pl.BlockSpec((tm, tn), lambda i, j: (i, j)),
        out_shape=jax.ShapeDtypeStruct((rows, D_MODEL), BF16),
        compiler_params=_params("arbitrary", "arbitrary"),
        name="merge",
    )(h, pa, pb, oc, wg_t, wg_t, wg_t, wba, wbb, wbc)


def _matmul_residual_kernel(a_ref, w_ref, res_ref, gate_ref, o_ref):
    w = w_ref[...]
    for rows in _row_blocks(o_ref.shape[0]):
        o_ref[rows, :] = res_ref[rows, :] + gate_ref[0] * _dot(a_ref[rows, :], w)


def _matmul_residual(a, w, res, mods, seg_of_tile, layer, which_gate, tm, tn):
    rows, depth = a.shape
    return pl.pallas_call(
        _matmul_residual_kernel,
        grid=(rows // tm, D_MODEL // tn),
        in_specs=[
            pl.BlockSpec((tm, depth), lambda i, j: (i, 0)),
            pl.BlockSpec((depth, tn), lambda i, j: (0, j)),
            pl.BlockSpec((tm, tn), lambda i, j: (i, j)),
            pl.BlockSpec((1, 1, tn), lambda i, j: (_mod_row(layer, seg_of_tile(i), which_gate), 0, j)),
        ],
        out_specs=pl.BlockSpec((tm, tn), lambda i, j: (i, j)),
        out_shape=jax.ShapeDtypeStruct((rows, D_MODEL), F32),
        compiler_params=_params("arbitrary", "arbitrary"),
        name="matmul_residual",
    )(a, w, res, mods)


def _matmul_residual_norm_kernel(a_ref, w_ref, res_ref, gate_ref, g_ref, sh_ref, sc_ref, x_ref, h_ref):
    a = a_ref[...]
    for n in range(0, D_MODEL, MXU_ROWS):
        cols = slice(n, n + MXU_ROWS)
        x_ref[:, cols] = res_ref[:, cols] + gate_ref[0, :, cols] * _dot(a, w_ref[:, cols])
    x = x_ref[...]
    y = x * lax.rsqrt(jnp.mean(x * x, axis=-1, keepdims=True) + EPS) * g_ref[0]
    h_ref[...] = (y * (1 + sc_ref[0]) + sh_ref[0]).astype(BF16)


def _matmul_residual_norm(a, w, res, mods, g, seg_of_tile, layer, which_gate, which_shift, tm):
    rows, depth = a.shape
    mod = lambda which: pl.BlockSpec((1, 1, D_MODEL), lambda i: (_mod_row(layer, seg_of_tile(i), which), 0, 0))
    row_spec = lambda width: pl.BlockSpec((tm, width), lambda i: (i, 0))
    return pl.pallas_call(
        _matmul_residual_norm_kernel,
        grid=(rows // tm,),
        in_specs=[
            row_spec(depth),
            pl.BlockSpec((depth, D_MODEL), lambda i: (0, 0), pipeline_mode=ONE_BUFFER),
            row_spec(D_MODEL),
            mod(which_gate),
            pl.BlockSpec((1, 1, D_MODEL), lambda i: (layer, 0, 0)),
            mod(which_shift), mod(which_shift + 1),
        ],
        out_specs=[row_spec(D_MODEL), row_spec(D_MODEL)],
        out_shape=[jax.ShapeDtypeStruct((rows, D_MODEL), F32), jax.ShapeDtypeStruct((rows, D_MODEL), BF16)],
        compiler_params=_params("arbitrary"),
        name="matmul_residual_norm",
    )(a, w, res, mods, g.reshape(DEPTH, 1, D_MODEL), mods, mods)


def _ffn_up_kernel(h_ref, wu_ref, wv_ref, cw_ref, *rest, seq, sub):
    if len(rest) == 3:
        cast_src_ref, o_ref, cast_dst_ref = rest
        cast_dst_ref[...] = cast_src_ref[...].astype(BF16)
    else:
        o_ref, = rest
    for n in range(0, o_ref.shape[1], sub):
        cols = slice(n, n + sub)
        cu = _conv3(_dot_rows(h_ref, wu_ref[:, cols].astype(BF16)), cw_ref.at[:, cols], seq)
        gate = _dot_rows(h_ref, wv_ref[:, cols].astype(BF16))
        o_ref[:, cols] = (cu * jax.nn.sigmoid(cu) * gate).astype(BF16)


def _ffn_up(h, w_up, conv, layer, seq, tm, tn, cast_src=None):
    rows = h.shape[0]
    nb = D_FF // tn
    in_specs = [
        pl.BlockSpec((tm, D_MODEL), lambda b, f: (b, 0)),
        pl.BlockSpec((None, D_MODEL, tn), lambda b, f: (layer, 0, f)),
        pl.BlockSpec((None, D_MODEL, tn), lambda b, f: (layer, 0, nb + f)),
        pl.BlockSpec((None, 3, tn), lambda b, f: (layer, 0, f)),
    ]
    out_specs = [pl.BlockSpec((tm, tn), lambda b, f: (b, f))]
    out_shape = [jax.ShapeDtypeStruct((rows, D_FF), BF16)]
    args = [h, w_up, w_up, conv]
    if cast_src is not None:
        src_spec, dst_spec, dst_shape = _cast_rider(cast_src, layer, (rows // tm) * nb, lambda b, f: b * nb + f)
        in_specs.append(src_spec)
        out_specs.append(dst_spec)
        out_shape.append(dst_shape)
        args.append(cast_src)
    outs = pl.pallas_call(
        functools.partial(_ffn_up_kernel, seq=seq, sub=min(tn, 256)),
        grid=(rows // tm, nb),
        in_specs=in_specs,
        out_specs=out_specs,
        out_shape=out_shape,
        compiler_params=_params("arbitrary", "arbitrary"),
        name="ffn_up",
    )(*args)
    return outs[0], (outs[1] if cast_src is not None else None)


_ROPE_SWAP = np.concatenate([np.arange(16, 32), np.arange(0, 16), np.arange(48, 64), np.arange(32, 48)])


def _rope_cos_sin(length):
    rows = length // GRID_W
    row = np.repeat(np.arange(rows, dtype=np.int32), GRID_W).astype(np.float32)
    col = np.tile(np.arange(GRID_W, dtype=np.int32), rows).astype(np.float32)
    inv = np.float32(ROPE_THETA) ** (-np.arange(0, ROPE_AXIS, 2, dtype=np.float32) / np.float32(ROPE_AXIS))
    ang_r = row[:, None] * inv[None]
    ang_c = col[:, None] * inv[None]
    cr, sr, cc, sc = np.cos(ang_r), np.sin(ang_r), np.cos(ang_c), np.sin(ang_c)
    return np.concatenate([cr, cr, cc, cc], axis=-1), np.concatenate([-sr, sr, -sc, sc], axis=-1)


def _identity_cos_sin(length):
    return np.ones((length, QK_ROPE), np.float32), np.zeros((length, QK_ROPE), np.float32)


def _rope_tables(cos, sin):
    return (jnp.asarray(np.concatenate([cos, cos, sin, sin], axis=-1), F32),
            jnp.asarray(np.concatenate([cos, sin], axis=-1), F32))


def _prepared_weights(w_in_t, w_uq, w_ukv, q_head_g, k_head_g):
    uq = jnp.transpose(w_uq, (0, 2, 1, 3)).reshape(DEPTH, N_HEADS // 2, 2, Q_LORA, QK_DIM)
    even, odd = uq[:, :, 0], uq[:, :, 1]
    swapped = lambda w: w[..., QK_NOPE:][..., _ROPE_SWAP]
    q_rope, k_rope_g = q_head_g[:, QK_NOPE:], k_head_g[:, QK_NOPE:]
    q_swap, k_swap = q_rope[:, _ROPE_SWAP], k_rope_g[:, _ROPE_SWAP]
    return dict(
        wg_t=w_in_t[:, OFF_G:, :].astype(BF16),
        wuq=jnp.concatenate([even[..., :QK_NOPE], odd[..., :QK_NOPE], even[..., QK_NOPE:], odd[..., QK_NOPE:],
                             swapped(even), swapped(odd)], axis=-1).astype(BF16),
        wukv=w_ukv.reshape(DEPTH, KV_LORA, N_HEADS * (QK_NOPE + V_DIM)).astype(BF16),
        q_gn=q_head_g[:, None, :QK_NOPE], k_gn=k_head_g[:, None, :QK_NOPE],
        q_g2=jnp.concatenate([q_rope, q_rope, q_swap, q_swap], axis=-1)[:, None],
        k_g2=jnp.concatenate([k_rope_g, k_swap], axis=-1)[:, None],
    )


def _tiles(batch, seq):
    rows = batch * seq
    return dict(
        tm=min(rows, 1024),
        seq_tm=seq if seq >= 1024 else rows,
        full_row_tm=min(rows, 512),
        proj_tm=min(seq, 512),
        q_tm=min(seq, 512),
        tq=min(seq, 512),
        attn_heads=max(1, min(N_HEADS, 4096 // seq)),
        narrow_tn=256,
        wide_tn=512,
    )


def kernel(x, c, ctx, c_ctx, norm1_g, norm2_g, w_mod, b_mod, w_in, pool_w, pool_scale, conv_w, q_lora_g, w_uq,
           kv_lora_g, w_ukv, q_head_g, k_head_g, w_branch_a, w_branch_b, w_branch_c, w_out, w_ffn_up, ffn_conv,
           w_ffn_down):
    batch, seq, _ = x.shape
    ctx_len = ctx.shape[1]
    assert batch + 1 <= MOD_ROWS
    cv = jnp.concatenate([c, c_ctx[None], jnp.zeros((MOD_ROWS - batch - 1, D_MODEL), F32)], axis=0)
    mods = _modulation(cv, w_mod, b_mod)
    w_in_t = jnp.swapaxes(w_in, 1, 2)
    pw = _prepared_weights(w_in_t, w_uq, w_ukv, q_head_g, k_head_g)

    def kv_proj(h, layer, cs, n_tok, tiles, bf16_w):
        kv, cast = _kv_proj(h, w_in_t, kv_lora_g, pw['wukv'], pw['k_gn'], pw['k_g2'], cs[1], layer, batch, n_tok,
                            tiles['proj_tm'], cast_src=None if 'w_out' in bf16_w else w_out)
        bf16_w.setdefault('w_out', cast)
        return kv

    def stream_layer(xs, layer, seg_of_tile, cs, n_tok, ctx_kv, tiles, bf16_w):
        tm, seq_tm = tiles['tm'], tiles['seq_tm']
        h = _norm_mod(xs, norm1_g, mods, seg_of_tile(tm), layer, 0, tm)
        pa = _branch_a(h, w_in_t, pool_w, pool_scale, layer, n_tok)
        pb = _branch_b(h, w_in_t, conv_w, layer, n_tok, seq_tm, tiles['narrow_tn'])
        q = _q_proj(h, w_in_t, q_lora_g, pw['wuq'], pw['q_gn'], pw['q_g2'], cs[0], layer, batch, n_tok,
                    tiles['q_tm'])
        kv = kv_proj(h, layer, cs, n_tok, tiles, bf16_w)
        kvs = [kv] if ctx_kv is None else [kv, ctx_kv]
        oc = _attention(q, kvs, tiles['tq'], tiles['attn_heads']).reshape(batch * n_tok, N_HEADS * V_DIM)
        merged = _merge(h, pa, pb, oc, pw['wg_t'], w_branch_a, w_branch_b, w_branch_c, layer, tm,
                        tiles['narrow_tn'])
        full_tm = tiles['full_row_tm']
        xs, h2 = _matmul_residual_norm(merged, bf16_w['w_out'], xs, mods, norm2_g, seg_of_tile(full_tm), layer,
                                       2, 3, full_tm)
        hidden, cast = _ffn_up(h2, w_ffn_up, ffn_conv, layer, n_tok, seq_tm, tiles['wide_tn'],
                               cast_src=None if 'w_down' in bf16_w else w_ffn_down)
        bf16_w.setdefault('w_down', cast)
        xs = _matmul_residual(hidden, bf16_w['w_down'], xs, mods, seg_of_tile(tm), layer, 5, tm, tiles['wide_tn'])
        return xs, kv

    cs_x = _rope_tables(*_rope_cos_sin(seq))
    cs_c = _rope_tables(*_identity_cos_sin(ctx_len))
    x_tiles = _tiles(batch, seq)
    c_tiles = _tiles(batch, ctx_len)
    x_seg = lambda tm: (lambda i: i // (seq // tm))
    c_seg = lambda tm: (lambda i: batch)

    xs = x.reshape(batch * seq, D_MODEL)
    cx = ctx.reshape(batch * ctx_len, D_MODEL)
    for layer in range(DEPTH):
        bf16_w = {}
        if layer == DEPTH - 1:
            hc = _norm_mod(cx, norm1_g, mods, c_seg(c_tiles['tm']), layer, 0, c_tiles['tm'])
            ctx_kv = kv_proj(hc, layer, cs_c, ctx_len, c_tiles, bf16_w)
        else:
            cx, ctx_kv = stream_layer(cx, layer, c_seg, cs_c, ctx_len, None, c_tiles, bf16_w)
        xs, _ = stream_layer(xs, layer, x_seg, cs_x, seq, tuple(ctx_kv), x_tiles, bf16_w)
    return xs.reshape(batch, seq, D_MODEL)
```
